```python
import jax, jax.numpy as jnp
from jax import lax
import numpy as np

D_MODEL = 2048
BATCH = 4
SEQ = 4096
DEPTH = 2

CTX_LEN = 256
GRID_W = 64
N_DIR = 2
MIX_WIDTH = D_MODEL
GLA_HEADS = 4
GLA_WIDTH = MIX_WIDTH // 2
GLA_DV = GLA_WIDTH // GLA_HEADS
GLA_DK = GLA_DV // 2
GLA_LR = 16
GLA_NORMALIZER = 16.0
MLSTM_HEADS = 4
MLSTM_WIDTH = MIX_WIDTH - GLA_WIDTH
MLSTM_DV = MLSTM_WIDTH // MLSTM_HEADS
MLSTM_DK = MLSTM_DV // 2
CHUNK = 64
D_FF = 128 * ((8 * D_MODEL // 3 + 127) // 128)
CONV_K = 3
EPS = 1e-6

PROJ_SIZES = (GLA_HEADS * GLA_DK, GLA_HEADS * GLA_DK, GLA_WIDTH, GLA_WIDTH, GLA_LR,
              MLSTM_HEADS * MLSTM_DK, MLSTM_HEADS * MLSTM_DK, MLSTM_WIDTH, MLSTM_WIDTH,
              N_DIR * 2 * MLSTM_HEADS)
PROJ_WIDTH = sum(PROJ_SIZES)
PROJ_SPLITS = tuple(int(s) for s in np.cumsum(PROJ_SIZES)[:-1])

kernel_name = 'hymba_gla_mlstm_convffn_prefix_block'


def rmsnorm(x, g):
    xf = x.astype(jnp.float32)
    y = xf * lax.rsqrt(jnp.mean(jnp.square(xf), axis=-1, keepdims=True) + EPS)
    return (y * g.astype(jnp.float32)).astype(x.dtype)


def _rev(t, d):
    return jnp.flip(t, axis=1) if d == 1 else t


def _chunk(t):
    b, t_len = t.shape[:2]
    t = t.reshape((b, t_len // CHUNK, CHUNK) + t.shape[2:])
    return jnp.swapaxes(jnp.moveaxis(t, 1, 0), 2, 3)


def _unchunk(t):
    n, b, h, c, d = t.shape
    return jnp.moveaxis(jnp.swapaxes(t, 2, 3), 0, 1).reshape(b, n * c, h, d)


def gla_scan(k, v, log_a, s0, q=None):
    tril = jnp.tril(jnp.ones((CHUNK, CHUNK), dtype=bool))
    xs = (_chunk(k), _chunk(v), _chunk(log_a)) + ((_chunk(q),) if q is not None else ())

    def step(s, inp):
        kc, vc, ac = inp[:3]
        bcum = jnp.cumsum(ac, axis=2)
        b_end = bcum[:, :, -1, :]
        k_to_end = kc * jnp.exp(b_end[:, :, None, :] - bcum)
        s_new = jnp.exp(b_end)[..., None] * s + jnp.einsum('bhik,bhiv->bhkv', k_to_end, vc)
        if q is None:
            return s_new, None
        qc = inp[3]
        rel = jnp.where(tril[:, :, None], bcum[:, :, :, None, :] - bcum[:, :, None, :, :], -jnp.inf)
        scores = jnp.einsum('bhjk,bhik,bhjik->bhji', qc, kc, jnp.exp(rel))
        o = (jnp.einsum('bhji,bhiv->bhjv', scores, vc)
             + jnp.einsum('bhjk,bhkv->bhjv', qc * jnp.exp(bcum), s))
        return s_new, o

    s_fin, o = lax.scan(step, s0, xs)
    return s_fin, (None if q is None else _unchunk(o))


def mlstm_scan(k, v, log_i, log_f, state0, q=None):
    tril = jnp.tril(jnp.ones((CHUNK, CHUNK), dtype=bool))
    xs = (_chunk(k), _chunk(v), _chunk(log_i), _chunk(log_f)) + ((_chunk(q),) if q is not None else ())

    def step(carry, inp):
        c, n, m = carry
        kc, vc, ic, fc = inp[:4]
        fcum = jnp.cumsum(fc, axis=-1)
        f_end = fcum[..., -1]
        w_log = f_end[..., None] - fcum + ic
        m_new = jnp.maximum(f_end + m, jnp.max(w_log, axis=-1))
        carry_scale = jnp.exp(f_end + m - m_new)
        w = jnp.exp(w_log - m_new[..., None])
        c_new = carry_scale[..., None, None] * c + jnp.einsum('bhi,bhik,bhiv->bhkv', w, kc, vc)
        n_new = carry_scale[..., None] * n + jnp.einsum('bhi,bhik->bhk', w, kc)
        if q is None:
            return (c_new, n_new, m_new), None
        qc = inp[4]
        d_log = jnp.where(tril, fcum[..., :, None] - fcum[..., None, :] + ic[..., None, :], -jnp.inf)
        inter_log = fcum + m[..., None]
        m_q = jnp.maximum(inter_log, jnp.max(d_log, axis=-1))
        s = jnp.einsum('bhjk,bhik->bhji', qc, kc) * jnp.exp(d_log - m_q[..., None])
        inter = jnp.exp(inter_log - m_q)
        num = (jnp.einsum('bhji,bhiv->bhjv', s, vc)
               + inter[..., None] * jnp.einsum('bhjk,bhkv->bhjv', qc, c))
        den = jnp.sum(s, axis=-1) + inter * jnp.einsum('bhjk,bhk->bhj', qc, n)
        h = num / jnp.maximum(jnp.abs(den), jnp.exp(-m_q))[..., None]
        return (c_new, n_new, m_new), h

    st_fin, h = lax.scan(step, state0, xs)
    return st_fin, (None if q is None else _unchunk(h))


def project_heads(h, w_in, gla_w_lr, gla_b_lr, mlstm_b_gate):
    b, t, _ = h.shape
    f32 = jnp.float32
    z = jnp.einsum('btd,dp->btp', h, w_in)
    gq, gk, gv, gg, glr, mq, mk, mv, mo, mgate = jnp.split(z, PROJ_SPLITS, axis=-1)
    gla_q = gq.reshape(b, t, GLA_HEADS, GLA_DK).astype(f32) * (GLA_DK ** -0.5)
    gla_k = gk.reshape(b, t, GLA_HEADS, GLA_DK).astype(f32)
    gla_v = gv.reshape(b, t, GLA_HEADS, GLA_DV).astype(f32)
    gla_gate = gg.reshape(b, t, GLA_HEADS, GLA_DV)
    dec = jnp.einsum('btr,zrk->zbtk', glr, gla_w_lr) + gla_b_lr[:, None, None, :]
    gla_loga = (jax.nn.log_sigmoid(dec.astype(f32)) / GLA_NORMALIZER).reshape(N_DIR, b, t, GLA_HEADS, GLA_DK)
    ml_q = mq.reshape(b, t, MLSTM_HEADS, MLSTM_DK).astype(f32)
    ml_k = mk.reshape(b, t, MLSTM_HEADS, MLSTM_DK).astype(f32) * (MLSTM_DK ** -0.5)
    ml_v = mv.reshape(b, t, MLSTM_HEADS, MLSTM_DV).astype(f32)
    ml_ogate = mo.reshape(b, t, MLSTM_HEADS, MLSTM_DV)
    pre = (mgate.reshape(b, t, N_DIR, 2, MLSTM_HEADS) + mlstm_b_gate).astype(f32)
    ml_logi = jnp.moveaxis(pre[:, :, :, 0], 2, 0)
    ml_logf = jnp.moveaxis(jax.nn.log_sigmoid(pre[:, :, :, 1]), 2, 0)
    scan_in = (gla_q, gla_k, gla_v, gla_loga, ml_q, ml_k, ml_v, ml_logi, ml_logf)
    return scan_in, (gla_gate, ml_ogate)


def recurrent_groups(lat, ctx, with_ctx_out):
    gq, gk, gv, ga, mq, mk, mv, mi, mf = lat
    cq, ck, cv, ca, cmq, cmk, cmv, cmi, cmf = ctx
    b = gq.shape[0]
    f32 = jnp.float32
    gla_lat, ml_lat, gla_ctx, ml_ctx = [], [], [], []
    for d in range(N_DIR):
        s0 = jnp.zeros((b, GLA_HEADS, GLA_DK, GLA_DV), f32)
        s_ctx, o_ctx = gla_scan(_rev(ck, d), _rev(cv, d), _rev(ca[d], d), s0,
                                _rev(cq, d) if with_ctx_out else None)
        _, o_lat = gla_scan(_rev(gk, d), _rev(gv, d), _rev(ga[d], d), s_ctx, _rev(gq, d))
        gla_lat.append(_rev(o_lat, d))
        st0 = (jnp.zeros((b, MLSTM_HEADS, MLSTM_DK, MLSTM_DV), f32),
               jnp.zeros((b, MLSTM_HEADS, MLSTM_DK), f32),
               jnp.zeros((b, MLSTM_HEADS), f32))
        st_ctx, h_ctx = mlstm_scan(_rev(cmk, d), _rev(cmv, d), _rev(cmi[d], d), _rev(cmf[d], d), st0,
                                   _rev(cmq, d) if with_ctx_out else None)
        _, h_lat = mlstm_scan(_rev(mk, d), _rev(mv, d), _rev(mi[d], d), _rev(mf[d], d), st_ctx, _rev(mq, d))
        ml_lat.append(_rev(h_lat, d))
        if with_ctx_out:
            gla_ctx.append(_rev(o_ctx, d))
            ml_ctx.append(_rev(h_ctx, d))
    lat_out = (gla_lat[0] + gla_lat[1], ml_lat[0] + ml_lat[1])
    ctx_out = (gla_ctx[0] + gla_ctx[1], ml_ctx[0] + ml_ctx[1]) if with_ctx_out else None
    return lat_out, ctx_out


def merge_groups(gla_o, ml_h, gla_gate, ml_ogate, gla_g_norm, mlstm_g_norm, w_out, dtype):
    b, t = gla_o.shape[:2]
    f32 = jnp.float32
    y_gla = rmsnorm(gla_o, gla_g_norm) * jax.nn.silu(gla_gate.astype(f32))
    y_ml = jax.nn.sigmoid(ml_ogate.astype(f32)) * rmsnorm(ml_h, mlstm_g_norm)
    y = jnp.concatenate([y_gla.reshape(b, t, GLA_WIDTH), y_ml.reshape(b, t, MLSTM_WIDTH)], axis=-1)
    return jnp.einsum('btm,md->btd', y.astype(dtype), w_out)


def conv_ffn(h, rows, cols, w_up, conv_w, conv_b, w_down):
    b, t, _ = h.shape
    u = jnp.einsum('btd,df->btf', h, w_up)
    u_gate, u_val = jnp.split(u, 2, axis=-1)
    g = lax.conv_general_dilated(u_gate.reshape(b, rows, cols, D_FF), conv_w[:, :, None, :],
                                 window_strides=(1, 1), padding='SAME',
                                 dimension_numbers=('NHWC', 'HWIO', 'NHWC'),
                                 feature_group_count=D_FF)
    g = g.reshape(b, t, D_FF) + conv_b
    return jnp.einsum('btf,fd->btd', jax.nn.silu(g) * u_val, w_down)


def trunk_layer(x, ctx, c_act, cc_act, w_mod, b_mod, g_norm1, g_norm2, w_in, gla_w_lr, gla_b_lr,
                mlstm_b_gate, gla_g_norm, mlstm_g_norm, w_out, w_up, conv_w, conv_b, w_down, update_ctx):
    b, t, _ = x.shape
    rows = t // GRID_W
    mod = (c_act @ w_mod + b_mod)[:, None, :]
    mod_c = (cc_act @ w_mod + b_mod)[None, None, :]
    sh1, sc1, gt1, sh2, sc2, gt2 = jnp.split(mod, 6, axis=-1)
    csh1, csc1, cgt1, csh2, csc2, cgt2 = jnp.split(mod_c, 6, axis=-1)

    h = rmsnorm(x, g_norm1) * (1 + sc1) + sh1
    hc = rmsnorm(ctx, g_norm1) * (1 + csc1) + csh1
    lat_in, lat_gates = project_heads(h, w_in, gla_w_lr, gla_b_lr, mlstm_b_gate)
    ctx_in, ctx_gates = project_heads(hc, w_in, gla_w_lr, gla_b_lr, mlstm_b_gate)
    (lat_gla, lat_ml), ctx_out = recurrent_groups(lat_in, ctx_in, update_ctx)
    x = x + gt1 * merge_groups(lat_gla, lat_ml, lat_gates[0], lat_gates[1],
                               gla_g_norm, mlstm_g_norm, w_out, x.dtype)

    h2 = rmsnorm(x, g_norm2) * (1 + sc2) + sh2
    x = x + gt2 * conv_ffn(h2, rows, GRID_W, w_up, conv_w, conv_b, w_down)

    if update_ctx:
        ctx = ctx + cgt1 * merge_groups(ctx_out[0], ctx_out[1], ctx_gates[0], ctx_gates[1],
                                        gla_g_norm, mlstm_g_norm, w_out, ctx.dtype)
        hc2 = rmsnorm(ctx, g_norm2) * (1 + csc2) + csh2
        ctx = ctx + cgt2 * conv_ffn(hc2, 1, ctx.shape[1], w_up, conv_w, conv_b, w_down)
    return x, ctx


def setup_inputs(seed: int = 0) -> dict:
    key = jax.random.key(seed)
    ks = jax.random.split(key, 20)
    f32 = jnp.float32
    L = DEPTH

    def nrm(k, shape, s):
        return jax.random.normal(k, shape, f32) * s

    gate_base = jnp.array([0.0, 3.0], f32)[None, None, :, None]
    return {
        'x': nrm(ks[0], (BATCH, SEQ, D_MODEL), 1.0),
        'c': nrm(ks[1], (BATCH, D_MODEL), 1.0),
        'ctx': nrm(ks[2], (BATCH, CTX_LEN, D_MODEL), 1.0),
        'c_ctx': nrm(ks[3], (D_MODEL,), 1.0),
        'w_mod': nrm(ks[4], (L, D_MODEL, 6 * D_MODEL), D_MODEL ** -0.5),
        'b_mod': nrm(ks[5], (L, 6 * D_MODEL), 0.01),
        'g_norm1': 1.0 + nrm(ks[6], (L, D_MODEL), 0.05),
        'g_norm2': 1.0 + nrm(ks[7], (L, D_MODEL), 0.05),
        'w_in': nrm(ks[8], (L, D_MODEL, PROJ_WIDTH), D_MODEL ** -0.5),
        'gla_w_lr': nrm(ks[9], (L, N_DIR, GLA_LR, GLA_HEADS * GLA_DK), GLA_LR ** -0.5),
        'gla_b_lr': nrm(ks[10], (L, N_DIR, GLA_HEADS * GLA_DK), 0.1),
        'mlstm_b_gate': gate_base + nrm(ks[11], (L, N_DIR, 2, MLSTM_HEADS), 0.1),
        'gla_g_norm': 1.0 + nrm(ks[12], (L, GLA_DV), 0.05),
        'mlstm_g_norm': 1.0 + nrm(ks[13], (L, MLSTM_DV), 0.05),
        'w_out': nrm(ks[14], (L, MIX_WIDTH, D_MODEL), MIX_WIDTH ** -0.5),
        'w_up': nrm(ks[15], (L, D_MODEL, 2 * D_FF), D_MODEL ** -0.5),
        'conv_w': nrm(ks[16], (L, CONV_K, CONV_K, D_FF), 1.0 / CONV_K),
        'conv_b': nrm(ks[17], (L, D_FF), 0.01),
        'w_down': nrm(ks[18], (L, D_FF, D_MODEL), D_FF ** -0.5),
        'g_final': 1.0 + nrm(ks[19], (D_MODEL,), 0.05),
    }


def reference(x, c, ctx, c_ctx, w_mod, b_mod, g_norm1, g_norm2, w_in, gla_w_lr, gla_b_lr, mlstm_b_gate,
              gla_g_norm, mlstm_g_norm, w_out, w_up, conv_w, conv_b, w_down, g_final):
    c_act = jax.nn.silu(c)
    cc_act = jax.nn.silu(c_ctx)
    for l in range(DEPTH):
        x, ctx = trunk_layer(x, ctx, c_act, cc_act, w_mod[l], b_mod[l], g_norm1[l], g_norm2[l], w_in[l],
                             gla_w_lr[l], gla_b_lr[l], mlstm_b_gate[l], gla_g_norm[l], mlstm_g_norm[l],
                             w_out[l], w_up[l], conv_w[l], conv_b[l], w_down[l],
                             update_ctx=(l < DEPTH - 1))
    return rmsnorm(x, g_final)
```

```python
import functools

import numpy as np
import jax
import jax.numpy as jnp
from jax import lax
from jax.experimental import pallas as pl
from jax.experimental.pallas import tpu as pltpu

D_MODEL = 2048
DEPTH = 2
GRID_W = 64
N_DIR = 2
HEADS = 4
DK = 128
DV = 256
GLA_LR = 16
GLA_NORMALIZER = 16.0
CHUNK = 64
D_FF = 5504
D_FF_PAD = 5632
FF_TILE = 512
EPS = 1e-6
N_LEVELS = 6
N_MOD = 6
MOD_ROWS = 8

VMEM_LIMIT = 56 * 1024 * 1024

F32 = jnp.float32
BF16 = jnp.bfloat16
NN = (((1,), (0,)), ((), ()))
NT = (((1,), (1,)), ((), ()))
TN = (((0,), (0,)), ((), ()))


def _mm(a, b, dims=NN):
    return lax.dot_general(a, b, dims, preferred_element_type=F32)


def _split(x):
    hi = x.astype(BF16)
    return hi, (x - hi.astype(F32)).astype(BF16)


def _mm3(a, b):
    ah, al = _split(a)
    bh, bl = _split(b)
    return _mm(ah, bh) + (_mm(ah, bl) + _mm(al, bh))


def _sigmoid(x):
    return 1.0 / (1.0 + jnp.exp(-x))


def _log_sigmoid(x):
    return jnp.minimum(x, 0.0) - jnp.log(1.0 + jnp.exp(-jnp.abs(x)))


def _rms(x, g):
    return x * lax.rsqrt(jnp.mean(x * x, axis=-1, keepdims=True) + EPS) * g


def _params(*sem):
    return pltpu.CompilerParams(dimension_semantics=sem, vmem_limit_bytes=VMEM_LIMIT)


def _mod_kernel(cc_ref, w_ref, b_ref, o_ref):
    a = cc_ref[...]
    a = a * _sigmoid(a)
    o_ref[...] = _mm3(a, w_ref[...]) + b_ref[...]


def _modulation(cc, w_mod, b_mod):
    tn = 1024
    out = pl.pallas_call(
        _mod_kernel,
        grid=(DEPTH, N_MOD * D_MODEL // tn),
        in_specs=[pl.BlockSpec((MOD_ROWS, D_MODEL), lambda l, j: (0, 0)),
                  pl.BlockSpec((None, D_MODEL, tn), lambda l, j: (l, 0, j)),
                  pl.BlockSpec((None, 1, tn), lambda l, j: (l, 0, j))],
        out_specs=pl.BlockSpec((None, MOD_ROWS, tn), lambda l, j: (l, 0, j)),
        out_shape=jax.ShapeDtypeStruct((DEPTH, MOD_ROWS, N_MOD * D_MODEL), F32),
        compiler_params=_params("arbitrary", "arbitrary"),
        name="modulation",
    )(cc, w_mod, b_mod.reshape(DEPTH, 1, N_MOD * D_MODEL))
    return out.reshape(DEPTH * MOD_ROWS * N_MOD, 1, D_MODEL)


def _mod_spec(layer, piece, seq, ctx_row, tm):
    base = layer * MOD_ROWS * N_MOD + piece
    assert seq % tm == 0
    blocks_per_batch = seq // tm

    def idx(i, *_):
        r = ctx_row if ctx_row is not None else i // blocks_per_batch
        return (base + r * N_MOD, 0, 0)

    return pl.BlockSpec((None, 1, D_MODEL), idx)


def _prenorm_kernel(x_ref, g_ref, sc_ref, sh_ref, o_ref):
    o_ref[...] = (_rms(x_ref[...], g_ref[...]) * (1.0 + sc_ref[...]) + sh_ref[...]).astype(o_ref.dtype)


def _prenorm(x, g, mod, layer, seq, ctx_row, tm):
    n = x.shape[0]
    return pl.pallas_call(
        _prenorm_kernel,
        grid=(n // tm,),
        in_specs=[pl.BlockSpec((tm, D_MODEL), lambda i: (i, 0)),
                  pl.BlockSpec((1, D_MODEL), lambda i: (0, 0)),
                  _mod_spec(layer, 1, seq, ctx_row, tm),
                  _mod_spec(layer, 0, seq, ctx_row, tm)],
        out_specs=pl.BlockSpec((tm, D_MODEL), lambda i: (i, 0)),
        out_shape=jax.ShapeDtypeStruct((n, D_MODEL), BF16),
        compiler_params=_params("arbitrary"),
        name="prenorm",
    )(x, g, mod, mod)


def _inproj_kernel(h_ref, w_ref, ws_ref, z_ref, zs_ref):
    h = h_ref[...]
    z_ref[...] = _mm(h, w_ref[...]).astype(z_ref.dtype)

    @pl.when(pl.program_id(1) == 0)
    def _():
        zs_ref[...] = _mm(h, ws_ref[...])


def _inproj(h, w_main, w_small, tm, tn):
    n = h.shape[0]
    p = w_main.shape[1]
    ps = w_small.shape[1]
    return pl.pallas_call(
        _inproj_kernel,
        grid=(n // tm, p // tn),
        in_specs=[pl.BlockSpec((tm, D_MODEL), lambda i, j: (i, 0)),
                  pl.BlockSpec((D_MODEL, tn), lambda i, j: (0, j)),
                  pl.BlockSpec((D_MODEL, ps), lambda i, j: (0, 0))],
        out_specs=[pl.BlockSpec((tm, tn), lambda i, j: (i, j)),
                   pl.BlockSpec((tm, ps), lambda i, j: (i, 0))],
        out_shape=[jax.ShapeDtypeStruct((n, p), BF16), jax.ShapeDtypeStruct((n, ps), F32)],
        compiler_params=_params("arbitrary", "arbitrary"),
        name="inproj",
    )(h, w_main, w_small)


def _matmul_kernel(h_ref, w_ref, z_ref):
    z_ref[...] = _mm(h_ref[...], w_ref[...]).astype(z_ref.dtype)


def _upproj(h, w, tm, tn):
    n = h.shape[0]
    p = w.shape[1]
    return pl.pallas_call(
        _matmul_kernel,
        grid=(n // tm, p // tn),
        in_specs=[pl.BlockSpec((tm, D_MODEL), lambda i, j: (i, 0)),
                  pl.BlockSpec((D_MODEL, tn), lambda i, j: (0, j))],
        out_specs=pl.BlockSpec((tm, tn), lambda i, j: (i, j)),
        out_shape=jax.ShapeDtypeStruct((n, p), BF16),
        compiler_params=_params("arbitrary", "arbitrary"),
        name="upproj",
    )(h, w)


def _outproj_kernel(yg_ref, ym_ref, wg_ref, wm_ref, x_ref, gt_ref, g2_ref, sc_ref, sh_ref, x1_ref, h2_ref):
    acc = _mm(yg_ref[...], wg_ref[...]) + _mm(ym_ref[...], wm_ref[...])
    x1 = x_ref[...] + gt_ref[...] * acc
    x1_ref[...] = x1
    h2_ref[...] = (_rms(x1, g2_ref[...]) * (1.0 + sc_ref[...]) + sh_ref[...]).astype(h2_ref.dtype)


def _outproj(yg, ym, w_out, x, g2, mod, layer, seq, ctx_row, tm):
    n = x.shape[0]
    half = w_out.shape[0] // 2
    row = lambda i: (i, 0)
    return pl.pallas_call(
        _outproj_kernel,
        grid=(n // tm,),
        in_specs=[pl.BlockSpec((tm, half), row),
                  pl.BlockSpec((tm, half), row),
                  pl.BlockSpec((half, D_MODEL), lambda i: (0, 0)),
                  pl.BlockSpec((half, D_MODEL), lambda i: (1, 0)),
                  pl.BlockSpec((tm, D_MODEL), row),
                  _mod_spec(layer, 2, seq, ctx_row, tm),
                  pl.BlockSpec((1, D_MODEL), lambda i: (0, 0)),
                  _mod_spec(layer, 4, seq, ctx_row, tm),
                  _mod_spec(layer, 3, seq, ctx_row, tm)],
        out_specs=[pl.BlockSpec((tm, D_MODEL), row), pl.BlockSpec((tm, D_MODEL), row)],
        out_shape=[jax.ShapeDtypeStruct((n, D_MODEL), F32), jax.ShapeDtypeStruct((n, D_MODEL), BF16)],
        compiler_params=_params("arbitrary"),
        name="outproj",
    )(yg, ym, w_out, w_out, x, mod, g2, mod, mod)


def _ffn_kernel(*refs, grid_w, vertical, final, tm, tf):
    if vertical:
        ug_ref, top_ref, bot_ref = refs[:3]
        refs = refs[3:]
    else:
        ug_ref = refs[0]
        refs = refs[1:]
    uv_ref, cw_ref, cb_ref, wd_ref, x1_ref, gt_ref, gn_ref = refs[:7]
    refs = refs[7:]
    if final:
        out_ref, acc_ref = refs
    else:
        sc_ref, sh_ref, x2_ref, hn_ref, acc_ref = refs
    i = pl.program_id(0)
    f = pl.program_id(1)
    nf = pl.num_programs(1)

    @pl.when(f == 0)
    def _():
        acc_ref[...] = jnp.zeros_like(acc_ref)

    cw = cw_ref[...]
    mid = ug_ref[...].astype(F32)
    if vertical:
        blocks_per_batch = (grid_w * grid_w) // tm
        top_ok = (i % blocks_per_batch != 0).astype(F32)
        bot_ok = (i % blocks_per_batch != blocks_per_batch - 1).astype(F32)
        ext = jnp.concatenate([top_ref[...].astype(F32) * top_ok, mid, bot_ref[...].astype(F32) * bot_ok], axis=0)
        up = ext[0:tm]
        down = ext[2 * grid_w:2 * grid_w + tm]
        taps = [cw[c:c + 1] * up + cw[3 + c:4 + c] * mid + cw[6 + c:7 + c] * down for c in range(3)]
    else:
        taps = [cw[3 + c:4 + c] * mid for c in range(3)]
    col = lax.broadcasted_iota(jnp.int32, (tm, tf), 0) & (grid_w - 1)
    left = jnp.where(col != 0, pltpu.roll(taps[0], 1, 0), 0.0)
    right = jnp.where(col != grid_w - 1, pltpu.roll(taps[2], tm - 1, 0), 0.0)
    g = taps[1] + left + right + cb_ref[...]
    act = (g * _sigmoid(g) * uv_ref[...].astype(F32)).astype(BF16)
    acc_ref[...] += _mm(act, wd_ref[...])

    @pl.when(f == nf - 1)
    def _():
        x2 = x1_ref[...] + gt_ref[...] * acc_ref[...]
        if final:
            out_ref[...] = _rms(x2, gn_ref[...])
        else:
            x2_ref[...] = x2
            hn_ref[...] = (_rms(x2, gn_ref[...]) * (1.0 + sc_ref[...]) + sh_ref[...]).astype(hn_ref.dtype)


def _ffn(u, conv_w9, conv_b, w_down, x1, mod, layer, gn, seq, ctx_row, grid_w, vertical, final, tm):
    n = x1.shape[0]
    tf = FF_TILE
    nf = D_FF_PAD // tf
    halo = grid_w
    hb = tm // halo
    n_halo_blocks = n // halo
    row = lambda i, f: (i, 0)
    in_specs = [pl.BlockSpec((tm, tf), lambda i, f: (i, f))]
    args = [u]
    if vertical:
        in_specs += [pl.BlockSpec((halo, tf), lambda i, f: (jnp.maximum(i * hb - 1, 0), f)),
                     pl.BlockSpec((halo, tf), lambda i, f: (jnp.minimum(i * hb + hb, n_halo_blocks - 1), f))]
        args += [u, u]
    in_specs += [pl.BlockSpec((tm, tf), lambda i, f: (i, nf + f)),
                 pl.BlockSpec((9, tf), lambda i, f: (0, f)),
                 pl.BlockSpec((1, tf), lambda i, f: (0, f)),
                 pl.BlockSpec((tf, D_MODEL), lambda i, f: (f, 0)),
                 pl.BlockSpec((tm, D_MODEL), row),
                 _mod_spec(layer, 5, seq, ctx_row, tm),
                 pl.BlockSpec((1, D_MODEL), lambda i, f: (0, 0))]
    args += [u, conv_w9, conv_b, w_down, x1, mod, gn]
    if final:
        out_specs = pl.BlockSpec((tm, D_MODEL), row)
        out_shape = jax.ShapeDtypeStruct((n, D_MODEL), F32)
    else:
        in_specs += [_mod_spec(layer + 1, 1, seq, ctx_row, tm),
                     _mod_spec(layer + 1, 0, seq, ctx_row, tm)]
        args += [mod, mod]
        out_specs = [pl.BlockSpec((tm, D_MODEL), row), pl.BlockSpec((tm, D_MODEL), row)]
        out_shape = [jax.ShapeDtypeStruct((n, D_MODEL), F32), jax.ShapeDtypeStruct((n, D_MODEL), BF16)]
    return pl.pallas_call(
        functools.partial(_ffn_kernel, grid_w=grid_w, vertical=vertical, final=final, tm=tm, tf=tf),
        grid=(n // tm, nf),
        in_specs=in_specs,
        out_specs=out_specs,
        out_shape=out_shape,
        scratch_shapes=[pltpu.VMEM((tm, D_MODEL), F32)],
        compiler_params=_params("arbitrary", "arbitrary"),
        name="ffn",
    )(*args)


def _flip2(a):
    return a[..., ::-1, ::-1]


def _gla_constants():
    c = CHUNK
    m = np.zeros((N_LEVELS + 2, c, c), np.float32)
    later = np.zeros((N_LEVELS, c, 1), np.float32)
    pair = np.zeros((N_LEVELS + 1, c, c), np.float32)
    for lvl in range(N_LEVELS):
        n = c >> lvl
        half = n // 2
        for r in range(c):
            bnd = (r // n) * n + half - 1
            if r % n >= half:
                m[lvl, r, bnd + 1:r + 1] = 1.0
                later[lvl, r] = 1.0
            else:
                m[lvl, r, r + 1:bnd + 1] = 1.0
        blk = np.arange(c) // n
        pos = np.arange(c) % n
        pair[lvl] = (blk[:, None] == blk[None, :]) & (pos[:, None] >= half) & (pos[None, :] < half)
    pair[N_LEVELS] = np.eye(c)
    for r in range(c):
        m[N_LEVELS, r, :r + 1] = 1.0
        m[N_LEVELS + 1, r, r + 1:] = 1.0
    m = np.stack([m, _flip2(m)]).reshape(N_DIR, (N_LEVELS + 2) * c, c)
    later = np.stack([later, later[:, ::-1]])
    later = np.broadcast_to(later, (N_DIR, N_LEVELS, c, DK))
    pair = np.stack([pair, _flip2(pair)])
    return jnp.asarray(m, BF16), jnp.asarray(later, F32), jnp.asarray(pair, F32)


def _mlstm_constants():
    tri = np.tril(np.ones((CHUNK, CHUNK), np.float32))
    return jnp.asarray(np.stack([tri, tri.T]), F32), jnp.asarray(np.eye(CHUNK), F32)


def _gla_chunk(d, qc, kc, vc, glr, wlr, blr, seg_ref, later_ref, pair_ref, st_ref, want_out):
    c = CHUNK
    la = _log_sigmoid(_mm3(glr, wlr) + blr) * (1.0 / GLA_NORMALIZER)
    la_hi, la_lo = _split(la)
    seg = seg_ref[d]
    e_all = jnp.exp(_mm(seg, la_hi) + _mm(seg, la_lo))
    e_cum = e_all[N_LEVELS * c:(N_LEVELS + 1) * c]
    e_end = e_all[(N_LEVELS + 1) * c:(N_LEVELS + 2) * c]
    last = c - 1 if d == 0 else 0
    decay = e_cum[last:last + 1]
    qf = qc.astype(F32) * (DK ** -0.5)
    kf = kc.astype(F32)
    st = st_ref[...]
    out = None
    if want_out:
        a = pair_ref[d, N_LEVELS] * _mm(qf.astype(BF16), kc, NT)
        for lvl in range(N_LEVELS):
            e = e_all[lvl * c:(lvl + 1) * c]
            later = later_ref[d, lvl]
            ql = (qf * e * later).astype(BF16)
            kl = (kf * e * (1.0 - later)).astype(BF16)
            a = a + pair_ref[d, lvl] * _mm(ql, kl, NT)
        out = _mm(a.astype(BF16), vc) + _mm((qf * e_cum).astype(BF16), st.astype(BF16), NT)
    st_ref[...] = st * decay + _mm(vc, (kf * e_end).astype(BF16), TN)
    return out


def _gla_kernel(*refs, with_ctx_out):
    (q_ref, k_ref, v_ref, g_ref, zs_ref, cq_ref, ck_ref, cv_ref, cg_ref, czs_ref,
     wlr_ref, blr_ref, gn_ref, seg_ref, later_ref, pair_ref) = refs[:16]
    if with_ctx_out:
        y_ref, cy_ref, st_ref, o_ref, co_ref = refs[16:]
    else:
        y_ref, st_ref, o_ref = refs[16:]
        cy_ref = co_ref = None
    n_lat = q_ref.shape[0] // CHUNK
    n_ctx = cq_ref.shape[0] // CHUNK
    gn = gn_ref[...]

    def finish(tot, gate):
        gate = gate.astype(F32)
        return (_rms(tot, gn) * (gate * _sigmoid(gate))).astype(BF16)

    for d in range(N_DIR):
        st_ref[...] = jnp.zeros_like(st_ref)

        def step(qr, kr, vr, zr, gr, acc_ref, yr, rows, want_out, d=d):
            o = _gla_chunk(d, qr[rows, :], kr[rows, :], vr[rows, :], zr[rows, 0:GLA_LR], wlr_ref[d], blr_ref[d],
                           seg_ref, later_ref, pair_ref, st_ref, want_out)
            if not want_out:
                return
            if d == 0:
                acc_ref[rows, :] = o
            else:
                yr[rows, :] = finish(acc_ref[rows, :] + o, gr[rows, :])

        for s in range(n_ctx):
            n = s if d == 0 else n_ctx - 1 - s
            step(cq_ref, ck_ref, cv_ref, czs_ref, cg_ref, co_ref, cy_ref, pl.ds(n * CHUNK, CHUNK), with_ctx_out)

        def body(s, carry, d=d, step=step):
            n = s if d == 0 else n_lat - 1 - s
            rows = pl.ds(pl.multiple_of(n * CHUNK, CHUNK), CHUNK)
            step(q_ref, k_ref, v_ref, zs_ref, g_ref, o_ref, y_ref, rows, True)
            return carry

        lax.fori_loop(0, n_lat, body, 0)


def _gla_scan(z, zs, cz, czs, w_lr, b_lr, g_norm, consts, batch, with_ctx_out):
    t = z.shape[0] // batch
    tc = cz.shape[0] // batch
    seg, later, pair = consts
    kq = DK // DK
    del kq
    lat = lambda cols, off: pl.BlockSpec((t, cols), lambda b, h, off=off: (b, off + h))
    ctx = lambda cols, off: pl.BlockSpec((tc, cols), lambda b, h, off=off: (b, off + h))
    const = lambda shape: pl.BlockSpec(shape, lambda b, h: (0,) * len(shape))
    in_specs = [lat(DK, 0), lat(DK, HEADS), lat(DV, HEADS), lat(DV, 2 * HEADS),
                pl.BlockSpec((t, zs.shape[1]), lambda b, h: (b, 0)),
                ctx(DK, 0), ctx(DK, HEADS), ctx(DV, HEADS), ctx(DV, 2 * HEADS),
                pl.BlockSpec((tc, zs.shape[1]), lambda b, h: (b, 0)),
                pl.BlockSpec((N_DIR, GLA_LR, DK), lambda b, h: (0, 0, h)),
                pl.BlockSpec((N_DIR, 1, DK), lambda b, h: (0, 0, h)),
                const((1, DV)), const(seg.shape), const(later.shape), const(pair.shape)]
    out_specs = [pl.BlockSpec((t, DV), lambda b, h: (b, h))]
    out_shape = [jax.ShapeDtypeStruct((batch * t, HEADS * DV), BF16)]
    scratch = [pltpu.VMEM((DV, DK), F32), pltpu.VMEM((t, DV), F32)]
    if with_ctx_out:
        out_specs.append(pl.BlockSpec((tc, DV), lambda b, h: (b, h)))
        out_shape.append(jax.ShapeDtypeStruct((batch * tc, HEADS * DV), BF16))
        scratch.append(pltpu.VMEM((tc, DV), F32))
    return pl.pallas_call(
        functools.partial(_gla_kernel, with_ctx_out=with_ctx_out),
        grid=(batch, HEADS),
        in_specs=in_specs,
        out_specs=out_specs,
        out_shape=out_shape,
        scratch_shapes=scratch,
        compiler_params=_params("arbitrary", "arbitrary"),
        name="gla_scan",
    )(z, z, z, z, zs, cz, cz, cz, cz, czs, w_lr, b_lr.reshape(N_DIR, 1, HEADS * DK), g_norm, seg, later, pair)


def _mlstm_chunk(d, qc, kc, vc, gates, tri, tri_t, eye, ct_ref, n_ref, m_ref, want_out):
    li_row = gates[2 * d:2 * d + 1]
    lf_row = _log_sigmoid(gates[2 * d + 1:2 * d + 2])
    li_col = jnp.sum(eye * li_row, axis=1, keepdims=True)
    lf_col = jnp.sum(eye * lf_row, axis=1, keepdims=True)
    fcum_col = jnp.sum(tri * lf_row, axis=1, keepdims=True)
    fcum_row = jnp.sum(tri_t * lf_col, axis=0, keepdims=True)
    f_end = jnp.sum(lf_row, axis=1, keepdims=True)
    m = m_ref[:, 0:1]
    m_new = jnp.maximum(f_end + m, jnp.max(f_end - fcum_row + li_row, axis=1, keepdims=True))
    carry = jnp.exp(f_end + m - m_new)
    w_col = jnp.exp(f_end - fcum_col + li_col - m_new)
    kf = kc.astype(F32) * (DK ** -0.5)
    kw = kf * w_col
    ct = ct_ref[...]
    nvec = n_ref[...]
    out = None
    if want_out:
        qf = qc.astype(F32)
        d_log = fcum_col - fcum_row + li_row
        causal = tri > 0.5
        inter_log = fcum_col + m
        m_q = jnp.maximum(inter_log, jnp.max(jnp.where(causal, d_log, -jnp.inf), axis=1, keepdims=True))
        p = jnp.where(causal, jnp.exp(d_log - m_q), 0.0)
        s = _mm(qc, kf.astype(BF16), NT) * p
        inter = jnp.exp(inter_log - m_q)
        num = _mm(s.astype(BF16), vc) + inter * _mm(qc, ct.astype(BF16), NT)
        den = jnp.sum(s, axis=1, keepdims=True) + inter * jnp.sum(qf * nvec, axis=1, keepdims=True)
        out = num / jnp.maximum(jnp.abs(den), jnp.exp(-m_q))
    ct_ref[...] = carry * ct + _mm(vc, kw.astype(BF16), TN)
    n_ref[...] = carry * nvec + jnp.sum(kw, axis=0, keepdims=True)
    m_ref[...] = jnp.broadcast_to(m_new, m_ref.shape)
    return out


def _mlstm_kernel(*refs, with_ctx_out):
    (q_ref, k_ref, v_ref, g_ref, gr_ref, cq_ref, ck_ref, cv_ref, cg_ref, cgr_ref,
     bias_ref, gn_ref, tri_ref, eye_ref) = refs[:14]
    if with_ctx_out:
        y_ref, cy_ref, ct_ref, n_ref, m_ref, o_ref, co_ref = refs[14:]
    else:
        y_ref, ct_ref, n_ref, m_ref, o_ref = refs[14:]
        cy_ref = co_ref = None
    n_lat = q_ref.shape[0] // CHUNK
    n_ctx = cq_ref.shape[0] // CHUNK
    gn = gn_ref[...]

    def finish(tot, gate):
        return (_sigmoid(gate.astype(F32)) * _rms(tot, gn)).astype(BF16)

    for d in range(N_DIR):
        ct_ref[...] = jnp.zeros_like(ct_ref)
        n_ref[...] = jnp.zeros_like(n_ref)
        m_ref[...] = jnp.zeros_like(m_ref)

        def step(qr, kr, vr, grr, gr, acc_ref, yr, n, rows, want_out, d=d):
            o = _mlstm_chunk(d, qr[rows, :], kr[rows, :], vr[rows, :], grr[n] + bias_ref[...],
                             tri_ref[d], tri_ref[1 - d], eye_ref[...], ct_ref, n_ref, m_ref, want_out)
            if not want_out:
                return
            if d == 0:
                acc_ref[rows, :] = o
            else:
                yr[rows, :] = finish(acc_ref[rows, :] + o, gr[rows, :])

        for s in range(n_ctx):
            n = s if d == 0 else n_ctx - 1 - s
            step(cq_ref, ck_ref, cv_ref, cgr_ref, cg_ref, co_ref, cy_ref, n, pl.ds(n * CHUNK, CHUNK), with_ctx_out)

        def body(s, carry, d=d, step=step):
            n = s if d == 0 else n_lat - 1 - s
            rows = pl.ds(pl.multiple_of(n * CHUNK, CHUNK), CHUNK)
            step(q_ref, k_ref, v_ref, gr_ref, g_ref, o_ref, y_ref, n, rows, True)
            return carry

        lax.fori_loop(0, n_lat, body, 0)


def _gate_rows(zs, batch):
    n = zs.shape[0]
    g = zs[:, GLA_LR:].reshape(n // CHUNK, CHUNK, N_DIR * 2, HEADS)
    del batch
    return jnp.transpose(g, (3, 0, 2, 1))


def _mlstm_scan(z, gr, cz, cgr, b_gate, g_norm, consts, batch, with_ctx_out):
    t = z.shape[0] // batch
    tc = cz.shape[0] // batch
    tri, eye = consts
    base = (2 * HEADS * DK + 2 * HEADS * DV)
    bq, bv = base // DK, (base + 2 * HEADS * DK) // DV
    lat = lambda cols, off: pl.BlockSpec((t, cols), lambda b, h, off=off: (b, off + h))
    ctx = lambda cols, off: pl.BlockSpec((tc, cols), lambda b, h, off=off: (b, off + h))
    const = lambda shape: pl.BlockSpec(shape, lambda b, h: (0,) * len(shape))
    in_specs = [lat(DK, bq), lat(DK, bq + HEADS), lat(DV, bv), lat(DV, bv + HEADS),
                pl.BlockSpec((None, t // CHUNK, 2 * N_DIR, CHUNK), lambda b, h: (h, b, 0, 0)),
                ctx(DK, bq), ctx(DK, bq + HEADS), ctx(DV, bv), ctx(DV, bv + HEADS),
                pl.BlockSpec((None, tc // CHUNK, 2 * N_DIR, CHUNK), lambda b, h: (h, b, 0, 0)),
                pl.BlockSpec((None, 2 * N_DIR, 1), lambda b, h: (h, 0, 0)),
                const((1, DV)), const(tri.shape), const(eye.shape)]
    out_specs = [pl.BlockSpec((t, DV), lambda b, h: (b, h))]
    out_shape = [jax.ShapeDtypeStruct((batch * t, HEADS * DV), BF16)]
    scratch = [pltpu.VMEM((DV, DK), F32), pltpu.VMEM((1, DK), F32), pltpu.VMEM((1, DK), F32),
               pltpu.VMEM((t, DV), F32)]
    if with_ctx_out:
        out_specs.append(pl.BlockSpec((tc, DV), lambda b, h: (b, h)))
        out_shape.append(jax.ShapeDtypeStruct((batch * tc, HEADS * DV), BF16))
        scratch.append(pltpu.VMEM((tc, DV), F32))
    bias = jnp.transpose(b_gate.reshape(2 * N_DIR, HEADS))[:, :, None]
    return pl.pallas_call(
        functools.partial(_mlstm_kernel, with_ctx_out=with_ctx_out),
        grid=(batch, HEADS),
        in_specs=in_specs,
        out_specs=out_specs,
        out_shape=out_shape,
        scratch_shapes=scratch,
        compiler_params=_params("arbitrary", "arbitrary"),
        name="mlstm_scan",
    )(z, z, z, z, gr, cz, cz, cz, cz, cgr, bias, g_norm, tri, eye)


def _prep_layer_weights(w_in, w_out, w_up, conv_w, conv_b, w_down):
    hk, hv = HEADS * DK, HEADS * DV
    glr0 = 2 * hk + 2 * hv
    ml0 = glr0 + GLA_LR
    gate0 = ml0 + 2 * hk + 2 * hv
    w_main = jnp.concatenate([w_in[:, :glr0], w_in[:, ml0:gate0]], axis=1).astype(BF16)
    w_small = jnp.concatenate([w_in[:, glr0:ml0], w_in[:, gate0:]], axis=1).astype(BF16)
    pad = D_FF_PAD - D_FF
    zc = jnp.zeros((D_MODEL, pad), w_up.dtype)
    w_up_p = jnp.concatenate([w_up[:, :D_FF], zc, w_up[:, D_FF:], zc], axis=1).astype(BF16)
    conv_w9 = jnp.pad(conv_w.reshape(9, D_FF), ((0, 0), (0, pad)))
    conv_bp = jnp.pad(conv_b.reshape(1, D_FF), ((0, 0), (0, pad)))
    w_down_p = jnp.pad(w_down, ((0, pad), (0, 0))).astype(BF16)
    return w_main, w_small, w_out.astype(BF16), w_up_p, conv_w9, conv_bp, w_down_p


def kernel(x, c, ctx, c_ctx, w_mod, b_mod, g_norm1, g_norm2, w_in, gla_w_lr, gla_b_lr, mlstm_b_gate,
           gla_g_norm, mlstm_g_norm, w_out, w_up, conv_w, conv_b, w_down, g_final):
    batch, seq, _ = x.shape
    ctx_len = ctx.shape[1]
    assert batch < MOD_ROWS and seq == GRID_W * GRID_W and ctx_len % CHUNK == 0
    tm_lat = 512
    tm_ctx = min(512, batch * ctx_len)
    cc = jnp.zeros((MOD_ROWS, D_MODEL), F32).at[:batch].set(c).at[batch].set(c_ctx)
    mod = _modulation(cc, w_mod, b_mod)
    gla_consts = _gla_constants()
    mlstm_consts = _mlstm_constants()

    xl = x.reshape(batch * seq, D_MODEL)
    xc = ctx.reshape(batch * ctx_len, D_MODEL)
    row = lambda v: v.reshape(1, -1)
    hl = _prenorm(xl, row(g_norm1[0]), mod, 0, seq, None, tm_lat)
    hc = _prenorm(xc, row(g_norm1[0]), mod, 0, tm_ctx, batch, tm_ctx)
    out = None
    for l in range(DEPTH):
        last = l == DEPTH - 1
        w_main, w_small, w_out_b, w_up_p, conv_w9, conv_bp, w_down_p = _prep_layer_weights(
            w_in[l], w_out[l], w_up[l], conv_w[l], conv_b[l], w_down[l])
        z, zs = _inproj(hl, w_main, w_small, 1024, 1024)
        cz, czs = _inproj(hc, w_main, w_small, tm_ctx, 1024)
        gla = _gla_scan(z, zs, cz, czs, gla_w_lr[l], gla_b_lr[l], row(gla_g_norm[l]), gla_consts, batch, not last)
        ml = _mlstm_scan(z, _gate_rows(zs, batch), cz, _gate_rows(czs, batch), mlstm_b_gate[l],
                         row(mlstm_g_norm[l]), mlstm_consts, batch, not last)
        x1, h2 = _outproj(gla[0], ml[0], w_out_b, xl, row(g_norm2[l]), mod, l, seq, None, 256)
        u = _upproj(h2, w_up_p, 1024, 1024)
        if last:
            out = _ffn(u, conv_w9, conv_bp, w_down_p, x1, mod, l, row(g_final), seq, None, GRID_W, True, True, tm_lat)
        else:
            xl, hl = _ffn(u, conv_w9, conv_bp, w_down_p, x1, mod, l, row(g_norm1[l + 1]), seq, None,
                          GRID_W, True, False, tm_lat)
            c1, hc2 = _outproj(gla[1], ml[1], w_out_b, xc, row(g_norm2[l]), mod, l, 256, batch, 256)
            cu = _upproj(hc2, w_up_p, tm_ctx, 1024)
            xc, hc = _ffn(cu, conv_w9, conv_bp, w_down_p, c1, mod, l, row(g_norm1[l + 1]), tm_ctx, batch,
                          ctx_len, False, False, tm_ctx)
    return out.reshape(batch, seq, D_MODEL)
```

```python
import functools

import numpy as np
import jax
import jax.numpy as jnp
from jax import lax
from jax.experimental import pallas as pl
from jax.experimental.pallas import tpu as pltpu

D_MODEL = 2048
DEPTH = 2
GRID_W = 64
N_DIR = 2
HEADS = 4
DK = 128
DV = 256
GLA_LR = 16
GLA_NORMALIZER = 16.0
CHUNK = 64
D_FF = 5504
D_FF_PAD = 5632
FF_TILE = 512
EPS = 1e-6
N_LEVELS = 6
N_MOD = 6
MOD_ROWS = 8

VMEM_LIMIT = 56 * 1024 * 1024

F32 = jnp.float32
BF16 = jnp.bfloat16
NN = (((1,), (0,)), ((), ()))
NT = (((1,), (1,)), ((), ()))
TN = (((0,), (0,)), ((), ()))


def _mm(a, b, dims=NN):
    return lax.dot_general(a, b, dims, preferred_element_type=F32)


def _split(x):
    hi = x.astype(BF16)
    return hi, (x - hi.astype(F32)).astype(BF16)


def _mm3(a, b):
    ah, al = _split(a)
    bh, bl = _split(b)
    return _mm(ah, bh) + (_mm(ah, bl) + _mm(al, bh))


def _sigmoid(x):
    return 1.0 / (1.0 + jnp.exp(-x))


def _log_sigmoid(x):
    return jnp.minimum(x, 0.0) - jnp.log(1.0 + jnp.exp(-jnp.abs(x)))


def _rms(x, g):
    return x * lax.rsqrt(jnp.mean(x * x, axis=-1, keepdims=True) + EPS) * g


def _params(*sem):
    return pltpu.CompilerParams(dimension_semantics=sem, vmem_limit_bytes=VMEM_LIMIT)


def _mod_kernel(cc_ref, w_ref, b_ref, o_ref):
    a = cc_ref[...]
    a = a * _sigmoid(a)
    o_ref[...] = _mm3(a, w_ref[...]) + b_ref[...]


def _modulation(cc, w_mod, b_mod):
    tn = 1024
    out = pl.pallas_call(
        _mod_kernel,
        grid=(DEPTH, N_MOD * D_MODEL // tn),
        in_specs=[pl.BlockSpec((MOD_ROWS, D_MODEL), lambda l, j: (0, 0)),
                  pl.BlockSpec((None, D_MODEL, tn), lambda l, j: (l, 0, j)),
                  pl.BlockSpec((None, 1, tn), lambda l, j: (l, 0, j))],
        out_specs=pl.BlockSpec((None, MOD_ROWS, tn), lambda l, j: (l, 0, j)),
        out_shape=jax.ShapeDtypeStruct((DEPTH, MOD_ROWS, N_MOD * D_MODEL), F32),
        compiler_params=_params("arbitrary", "arbitrary"),
        name="modulation",
    )(cc, w_mod, b_mod.reshape(DEPTH, 1, N_MOD * D_MODEL))
    return out.reshape(DEPTH * MOD_ROWS * N_MOD, 1, D_MODEL)


def _mod_spec(layer, piece, seq, ctx_row, tm):
    base = layer * MOD_ROWS * N_MOD + piece
    assert seq % tm == 0
    blocks_per_batch = seq // tm

    def idx(i, *_):
        r = ctx_row if ctx_row is not None else i // blocks_per_batch
        return (base + r * N_MOD, 0, 0)

    return pl.BlockSpec((None, 1, D_MODEL), idx)


def _prenorm_kernel(x_ref, g_ref, sc_ref, sh_ref, o_ref):
    o_ref[...] = (_rms(x_ref[...], g_ref[...]) * (1.0 + sc_ref[...]) + sh_ref[...]).astype(o_ref.dtype)


def _prenorm(x, g, mod, layer, seq, ctx_row, tm):
    n = x.shape[0]
    return pl.pallas_call(
        _prenorm_kernel,
        grid=(n // tm,),
        in_specs=[pl.BlockSpec((tm, D_MODEL), lambda i: (i, 0)),
                  pl.BlockSpec((1, D_MODEL), lambda i: (0, 0)),
                  _mod_spec(layer, 1, seq, ctx_row, tm),
                  _mod_spec(layer, 0, seq, ctx_row, tm)],
        out_specs=pl.BlockSpec((tm, D_MODEL), lambda i: (i, 0)),
        out_shape=jax.ShapeDtypeStruct((n, D_MODEL), BF16),
        compiler_params=_params("arbitrary"),
        name="prenorm",
    )(x, g, mod, mod)


def _inproj_kernel(h_ref, w_ref, ws_ref, z_ref, zs_ref):
    h = h_ref[...]
    z_ref[...] = _mm(h, w_ref[...]).astype(z_ref.dtype)

    @pl.when(pl.program_id(1) == 0)
    def _():
        zs_ref[...] = _mm(h, ws_ref[...])


def _inproj(h, w_main, w_small, tm, tn):
    n = h.shape[0]
    p = w_main.shape[1]
    ps = w_small.shape[1]
    return pl.pallas_call(
        _inproj_kernel,
        grid=(n // tm, p // tn),
        in_specs=[pl.BlockSpec((tm, D_MODEL), lambda i, j: (i, 0)),
                  pl.BlockSpec((D_MODEL, tn), lambda i, j: (0, j)),
                  pl.BlockSpec((D_MODEL, ps), lambda i, j: (0, 0))],
        out_specs=[pl.BlockSpec((tm, tn), lambda i, j: (i, j)),
                   pl.BlockSpec((tm, ps), lambda i, j: (i, 0))],
        out_shape=[jax.ShapeDtypeStruct((n, p), BF16), jax.ShapeDtypeStruct((n, ps), F32)],
        compiler_params=_params("arbitrary", "arbitrary"),
        name="inproj",
    )(h, w_main, w_small)


def _matmul_kernel(h_ref, w_ref, z_ref):
    z_ref[...] = _mm(h_ref[...], w_ref[...]).astype(z_ref.dtype)


def _upproj(h, w, tm, tn):
    n = h.shape[0]
    p = w.shape[1]
    return pl.pallas_call(
        _matmul_kernel,
        grid=(n // tm, p // tn),
        in_specs=[pl.BlockSpec((tm, D_MODEL), lambda i, j: (i, 0)),
                  pl.BlockSpec((D_MODEL, tn), lambda i, j: (0, j))],
        out_specs=pl.BlockSpec((tm, tn), lambda i, j: (i, j)),
        out_shape=jax.ShapeDtypeStruct((n, p), BF16),
        compiler_params=_params("arbitrary", "arbitrary"),
        name="upproj",
    )(h, w)


def _outproj_kernel(yg_ref, ym_ref, wg_ref, wm_ref, x_ref, gt_ref, g2_ref, sc_ref, sh_ref, x1_ref, h2_ref):
    acc = _mm(yg_ref[...], wg_ref[...]) + _mm(ym_ref[...], wm_ref[...])
    x1 = x_ref[...] + gt_ref[...] * acc
    x1_ref[...] = x1
    h2_ref[...] = (_rms(x1, g2_ref[...]) * (1.0 + sc_ref[...]) + sh_ref[...]).astype(h2_ref.dtype)


def _outproj(yg, ym, w_out, x, g2, mod, layer, seq, ctx_row, tm):
    n = x.shape[0]
    half = w_out.shape[0] // 2
    row = lambda i: (i, 0)
    return pl.pallas_call(
        _outproj_kernel,
        grid=(n // tm,),
        in_specs=[pl.BlockSpec((tm, half), row),
                  pl.BlockSpec((tm, half), row),
                  pl.BlockSpec((half, D_MODEL), lambda i: (0, 0)),
                  pl.BlockSpec((half, D_MODEL), lambda i: (1, 0)),
                  pl.BlockSpec((tm, D_MODEL), row),
                  _mod_spec(layer, 2, seq, ctx_row, tm),
                  pl.BlockSpec((1, D_MODEL), lambda i: (0, 0)),
                  _mod_spec(layer, 4, seq, ctx_row, tm),
                  _mod_spec(layer, 3, seq, ctx_row, tm)],
        out_specs=[pl.BlockSpec((tm, D_MODEL), row), pl.BlockSpec((tm, D_MODEL), row)],
        out_shape=[jax.ShapeDtypeStruct((n, D_MODEL), F32), jax.ShapeDtypeStruct((n, D_MODEL), BF16)],
        compiler_params=_params("arbitrary"),
        name="outproj",
    )(yg, ym, w_out, w_out, x, mod, g2, mod, mod)


def _ffn_kernel(*refs, grid_w, vertical, final, tm, tf):
    if vertical:
        ug_ref, top_ref, bot_ref = refs[:3]
        refs = refs[3:]
    else:
        ug_ref = refs[0]
        refs = refs[1:]
    uv_ref, cw_ref, cb_ref, wd_ref, x1_ref, gt_ref, gn_ref = refs[:7]
    refs = refs[7:]
    if final:
        out_ref, acc_ref = refs
    else:
        sc_ref, sh_ref, x2_ref, hn_ref, acc_ref = refs
    i = pl.program_id(0)
    f = pl.program_id(1)
    nf = pl.num_programs(1)

    @pl.when(f == 0)
    def _():
        acc_ref[...] = jnp.zeros_like(acc_ref)

    cw = cw_ref[...]
    mid = ug_ref[...].astype(F32)
    if vertical:
        blocks_per_batch = (grid_w * grid_w) // tm
        top_ok = (i % blocks_per_batch != 0).astype(F32)
        bot_ok = (i % blocks_per_batch != blocks_per_batch - 1).astype(F32)
        ext = jnp.concatenate([top_ref[...].astype(F32) * top_ok, mid, bot_ref[...].astype(F32) * bot_ok], axis=0)
        up = ext[0:tm]
        down = ext[2 * grid_w:2 * grid_w + tm]
        taps = [cw[c:c + 1] * up + cw[3 + c:4 + c] * mid + cw[6 + c:7 + c] * down for c in range(3)]
    else:
        taps = [cw[3 + c:4 + c] * mid for c in range(3)]
    col = lax.broadcasted_iota(jnp.int32, (tm, tf), 0) & (grid_w - 1)
    left = jnp.where(col != 0, pltpu.roll(taps[0], 1, 0), 0.0)
    right = jnp.where(col != grid_w - 1, pltpu.roll(taps[2], tm - 1, 0), 0.0)
    g = taps[1] + left + right + cb_ref[...]
    act = (g * _sigmoid(g) * uv_ref[...].astype(F32)).astype(BF16)
    acc_ref[...] += _mm(act, wd_ref[...])

    @pl.when(f == nf - 1)
    def _():
        x2 = x1_ref[...] + gt_ref[...] * acc_ref[...]
        if final:
            out_ref[...] = _rms(x2, gn_ref[...])
        else:
            x2_ref[...] = x2
            hn_ref[...] = (_rms(x2, gn_ref[...]) * (1.0 + sc_ref[...]) + sh_ref[...]).astype(hn_ref.dtype)


def _ffn(u, conv_w9, conv_b, w_down, x1, mod, layer, gn, seq, ctx_row, grid_w, vertical, final, tm):
    n = x1.shape[0]
    tf = FF_TILE
    nf = D_FF_PAD // tf
    halo = grid_w
    hb = tm // halo
    n_halo_blocks = n // halo
    row = lambda i, f: (i, 0)
    in_specs = [pl.BlockSpec((tm, tf), lambda i, f: (i, f))]
    args = [u]
    if vertical:
        in_specs += [pl.BlockSpec((halo, tf), lambda i, f: (jnp.maximum(i * hb - 1, 0), f)),
                     pl.BlockSpec((halo, tf), lambda i, f: (jnp.minimum(i * hb + hb, n_halo_blocks - 1), f))]
        args += [u, u]
    in_specs += [pl.BlockSpec((tm, tf), lambda i, f: (i, nf + f)),
                 pl.BlockSpec((9, tf), lambda i, f: (0, f)),
                 pl.BlockSpec((1, tf), lambda i, f: (0, f)),
                 pl.BlockSpec((tf, D_MODEL), lambda i, f: (f, 0)),
                 pl.BlockSpec((tm, D_MODEL), row),
                 _mod_spec(layer, 5, seq, ctx_row, tm),
                 pl.BlockSpec((1, D_MODEL), lambda i, f: (0, 0))]
    args += [u, conv_w9, conv_b, w_down, x1, mod, gn]
    if final:
        out_specs = pl.BlockSpec((tm, D_MODEL), row)
        out_shape = jax.ShapeDtypeStruct((n, D_MODEL), F32)
    else:
        in_specs += [_mod_spec(layer + 1, 1, seq, ctx_row, tm),
                     _mod_spec(layer + 1, 0, seq, ctx_row, tm)]
        args += [mod, mod]
        out_specs = [pl.BlockSpec((tm, D_MODEL), row), pl.BlockSpec((tm, D_MODEL), row)]
        out_shape = [jax.ShapeDtypeStruct((n, D_MODEL), F32), jax.ShapeDtypeStruct((n, D_MODEL), BF16)]
    return pl.pallas_call(
        functools.partial(_ffn_kernel, grid_w=grid_w, vertical=vertical, final=final, tm=tm, tf=tf),
        grid=(n // tm, nf),
        in_specs=in_specs,
        out_specs=out_specs,
        out_shape=out_shape,
        scratch_shapes=[pltpu.VMEM((tm, D_MODEL), F32)],
        compiler_params=_params("arbitrary", "arbitrary"),
        name="ffn",
    )(*args)


def _flip2(a):
    return a[..., ::-1, ::-1]


def _gla_constants():
    c = CHUNK
    m = np.zeros((N_LEVELS + 2, c, c), np.float32)
    later = np.zeros((N_LEVELS, c, 1), np.float32)
    pair = np.zeros((N_LEVELS + 1, c, c), np.float32)
    for lvl in range(N_LEVELS):
        n = c >> lvl
        half = n // 2
        for r in range(c):
            bnd = (r // n) * n + half - 1
            if r % n >= half:
                m[lvl, r, bnd + 1:r + 1] = 1.0
                later[lvl, r] = 1.0
            else:
                m[lvl, r, r + 1:bnd + 1] = 1.0
        blk = np.arange(c) // n
        pos = np.arange(c) % n
        pair[lvl] = (blk[:, None] == blk[None, :]) & (pos[:, None] >= half) & (pos[None, :] < half)
    pair[N_LEVELS] = np.eye(c)
    for r in range(c):
        m[N_LEVELS, r, :r + 1] = 1.0
        m[N_LEVELS + 1, r, r + 1:] = 1.0
    m = np.stack([m, _flip2(m)]).reshape(N_DIR, (N_LEVELS + 2) * c, c)
    later = np.stack([later, later[:, ::-1]])
    later = np.broadcast_to(later, (N_DIR, N_LEVELS, c, DK))
    pair = np.stack([pair, _flip2(pair)])
    return jnp.asarray(m, BF16), jnp.asarray(later, F32), jnp.asarray(pair, F32)


def _mlstm_constants():
    tri = np.tril(np.ones((CHUNK, CHUNK), np.float32))
    return jnp.asarray(np.stack([tri, tri.T]), F32), jnp.asarray(np.eye(CHUNK), F32)


def _drive_scan(n_ctx, n_lat, with_ctx_out, chunk, finish, lat_io, ctx_io):
    assert n_ctx % 2 == 0 and n_lat % 2 == 0

    def one(d, is_ctx, s, n_total, second_half, want_out):
        gate_ref, acc_ref, y_ref = ctx_io if is_ctx else lat_io
        n = s if d == 0 else n_total - 1 - s
        start = n * CHUNK if isinstance(n, int) else pl.multiple_of(n * CHUNK, CHUNK)
        rows = pl.ds(start, CHUNK)
        o = yield from chunk(d, is_ctx, n, rows, want_out)
        if not want_out:
            return
        if second_half:
            y_ref[rows, :] = finish(acc_ref[rows, :] + o, gate_ref[rows, :])
        else:
            acc_ref[rows, :] = o

    def pair(*args):
        live = [one(d, *args) for d in range(N_DIR)]
        while live:
            for g in list(live):
                try:
                    next(g)
                except StopIteration:
                    live.remove(g)

    for s in range(n_ctx):
        pair(True, s, n_ctx, s >= n_ctx // 2, with_ctx_out)
    for half in range(2):
        def body(s, carry, half=half):
            pair(False, s, n_lat, half == 1, True)
            return carry

        lax.fori_loop(half * (n_lat // 2), (half + 1) * (n_lat // 2), body, 0)


def _gla_chunk(d, qc, kc, vc, glr, wlr, blr, seg_ref, later_ref, pair_ref, st_ref, want_out):
    c = CHUNK
    la = _log_sigmoid(_mm3(glr, wlr) + blr) * (1.0 / GLA_NORMALIZER)
    yield
    la_hi, la_lo = _split(la)
    seg = seg_ref[d]
    e_all = jnp.exp(_mm(seg, la_hi) + _mm(seg, la_lo))
    yield
    e_cum = e_all[N_LEVELS * c:(N_LEVELS + 1) * c]
    e_end = e_all[(N_LEVELS + 1) * c:(N_LEVELS + 2) * c]
    last = c - 1 if d == 0 else 0
    decay = e_cum[last:last + 1]
    qf = qc.astype(F32) * (DK ** -0.5)
    kf = kc.astype(F32)
    st = st_ref[d]
    out = None
    if want_out:
        a = pair_ref[d, N_LEVELS] * _mm(qf.astype(BF16), kc, NT)
        for lvl in range(N_LEVELS):
            e = e_all[lvl * c:(lvl + 1) * c]
            later = later_ref[d, lvl]
            ql = (qf * e * later).astype(BF16)
            kl = (kf * e * (1.0 - later)).astype(BF16)
            a = a + pair_ref[d, lvl] * _mm(ql, kl, NT)
            yield
        out = _mm(a.astype(BF16), vc) + _mm((qf * e_cum).astype(BF16), st.astype(BF16), NT)
    st_ref[d] = st * decay + _mm(vc, (kf * e_end).astype(BF16), TN)
    return out


def _gla_kernel(*refs, with_ctx_out):
    (q_ref, k_ref, v_ref, g_ref, zs_ref, cq_ref, ck_ref, cv_ref, cg_ref, czs_ref,
     wlr_ref, blr_ref, gn_ref, seg_ref, later_ref, pair_ref) = refs[:16]
    if with_ctx_out:
        y_ref, cy_ref, st_ref, o_ref, co_ref = refs[16:]
    else:
        y_ref, st_ref, o_ref = refs[16:]
        cy_ref = co_ref = None
    gn = gn_ref[...]

    def finish(tot, gate):
        gate = gate.astype(F32)
        return (_rms(tot, gn) * (gate * _sigmoid(gate))).astype(BF16)

    def chunk(d, is_ctx, n, rows, want_out):
        del n
        qr, kr, vr, zr = (cq_ref, ck_ref, cv_ref, czs_ref) if is_ctx else (q_ref, k_ref, v_ref, zs_ref)
        return (yield from _gla_chunk(d, qr[rows, :], kr[rows, :], vr[rows, :], zr[rows, 0:GLA_LR], wlr_ref[d],
                                      blr_ref[d], seg_ref, later_ref, pair_ref, st_ref, want_out))

    st_ref[...] = jnp.zeros_like(st_ref)
    _drive_scan(cq_ref.shape[0] // CHUNK, q_ref.shape[0] // CHUNK, with_ctx_out, chunk, finish,
                (g_ref, o_ref, y_ref), (cg_ref, co_ref, cy_ref))


def _gla_scan(z, zs, cz, czs, w_lr, b_lr, g_norm, consts, batch, with_ctx_out):
    t = z.shape[0] // batch
    tc = cz.shape[0] // batch
    seg, later, pair = consts
    kq = DK // DK
    del kq
    lat = lambda cols, off: pl.BlockSpec((t, cols), lambda b, h, off=off: (b, off + h))
    ctx = lambda cols, off: pl.BlockSpec((tc, cols), lambda b, h, off=off: (b, off + h))
    const = lambda shape: pl.BlockSpec(shape, lambda b, h: (0,) * len(shape))
    in_specs = [lat(DK, 0), lat(DK, HEADS), lat(DV, HEADS), lat(DV, 2 * HEADS),
                pl.BlockSpec((t, zs.shape[1]), lambda b, h: (b, 0)),
                ctx(DK, 0), ctx(DK, HEADS), ctx(DV, HEADS), ctx(DV, 2 * HEADS),
                pl.BlockSpec((tc, zs.shape[1]), lambda b, h: (b, 0)),
                pl.BlockSpec((N_DIR, GLA_LR, DK), lambda b, h: (0, 0, h)),
                pl.BlockSpec((N_DIR, 1, DK), lambda b, h: (0, 0, h)),
                const((1, DV)), const(seg.shape), const(later.shape), const(pair.shape)]
    out_specs = [pl.BlockSpec((t, DV), lambda b, h: (b, h))]
    out_shape = [jax.ShapeDtypeStruct((batch * t, HEADS * DV), BF16)]
    scratch = [pltpu.VMEM((N_DIR, DV, DK), F32), pltpu.VMEM((t, DV), F32)]
    if with_ctx_out:
        out_specs.append(pl.BlockSpec((tc, DV), lambda b, h: (b, h)))
        out_shape.append(jax.ShapeDtypeStruct((batch * tc, HEADS * DV), BF16))
        scratch.append(pltpu.VMEM((tc, DV), F32))
    return pl.pallas_call(
        functools.partial(_gla_kernel, with_ctx_out=with_ctx_out),
        grid=(batch, HEADS),
        in_specs=in_specs,
        out_specs=out_specs,
        out_shape=out_shape,
        scratch_shapes=scratch,
        compiler_params=_params("arbitrary", "arbitrary"),
        name="gla_scan",
    )(z, z, z, z, zs, cz, cz, cz, cz, czs, w_lr, b_lr.reshape(N_DIR, 1, HEADS * DK), g_norm, seg, later, pair)


def _mlstm_chunk(d, qc, kc, vc, gates, tri, tri_t, eye, ct_ref, n_ref, m_ref, want_out):
    li_row = gates[2 * d:2 * d + 1]
    lf_row = _log_sigmoid(gates[2 * d + 1:2 * d + 2])
    li_col = jnp.sum(eye * li_row, axis=1, keepdims=True)
    lf_col = jnp.sum(eye * lf_row, axis=1, keepdims=True)
    fcum_col = jnp.sum(tri * lf_row, axis=1, keepdims=True)
    fcum_row = jnp.sum(tri_t * lf_col, axis=0, keepdims=True)
    f_end = jnp.sum(lf_row, axis=1, keepdims=True)
    yield
    m = m_ref[d][:, 0:1]
    m_new = jnp.maximum(f_end + m, jnp.max(f_end - fcum_row + li_row, axis=1, keepdims=True))
    carry = jnp.exp(f_end + m - m_new)
    w_col = jnp.exp(f_end - fcum_col + li_col - m_new)
    kf = kc.astype(F32) * (DK ** -0.5)
    kw = kf * w_col
    yield
    ct = ct_ref[d]
    nvec = n_ref[d]
    out = None
    if want_out:
        qf = qc.astype(F32)
        d_log = fcum_col - fcum_row + li_row
        causal = tri > 0.5
        inter_log = fcum_col + m
        m_q = jnp.maximum(inter_log, jnp.max(jnp.where(causal, d_log, -jnp.inf), axis=1, keepdims=True))
        p = jnp.where(causal, jnp.exp(d_log - m_q), 0.0)
        s = _mm(qc, kf.astype(BF16), NT) * p
        inter = jnp.exp(inter_log - m_q)
        yield
        num = _mm(s.astype(BF16), vc) + inter * _mm(qc, ct.astype(BF16), NT)
        den = jnp.sum(s, axis=1, keepdims=True) + inter * jnp.sum(qf * nvec, axis=1, keepdims=True)
        out = num / jnp.maximum(jnp.abs(den), jnp.exp(-m_q))
    ct_ref[d] = carry * ct + _mm(vc, kw.astype(BF16), TN)
    n_ref[d] = carry * nvec + jnp.sum(kw, axis=0, keepdims=True)
    m_ref[d] = jnp.broadcast_to(m_new, nvec.shape)
    return out


def _mlstm_kernel(*refs, with_ctx_out):
    (q_ref, k_ref, v_ref, g_ref, gr_ref, cq_ref, ck_ref, cv_ref, cg_ref, cgr_ref,
     bias_ref, gn_ref, tri_ref, eye_ref) = refs[:14]
    if with_ctx_out:
        y_ref, cy_ref, ct_ref, n_ref, m_ref, o_ref, co_ref = refs[14:]
    else:
        y_ref, ct_ref, n_ref, m_ref, o_ref = refs[14:]
        cy_ref = co_ref = None
    gn = gn_ref[...]

    def finish(tot, gate):
        return (_sigmoid(gate.astype(F32)) * _rms(tot, gn)).astype(BF16)

    def chunk(d, is_ctx, n, rows, want_out):
        qr, kr, vr, grr = (cq_ref, ck_ref, cv_ref, cgr_ref) if is_ctx else (q_ref, k_ref, v_ref, gr_ref)
        return (yield from _mlstm_chunk(d, qr[rows, :], kr[rows, :], vr[rows, :], grr[n] + bias_ref[...],
                                        tri_ref[d], tri_ref[1 - d], eye_ref[...], ct_ref, n_ref, m_ref, want_out))

    ct_ref[...] = jnp.zeros_like(ct_ref)
    n_ref[...] = jnp.zeros_like(n_ref)
    m_ref[...] = jnp.zeros_like(m_ref)
    _drive_scan(cq_ref.shape[0] // CHUNK, q_ref.shape[0] // CHUNK, with_ctx_out, chunk, finish,
                (g_ref, o_ref, y_ref), (cg_ref, co_ref, cy_ref))


def _gate_rows(zs, batch):
    n = zs.shape[0]
    g = zs[:, GLA_LR:].reshape(n // CHUNK, CHUNK, N_DIR * 2, HEADS)
    del batch
    return jnp.transpose(g, (3, 0, 2, 1))


def _mlstm_scan(z, gr, cz, cgr, b_gate, g_norm, consts, batch, with_ctx_out):
    t = z.shape[0] // batch
    tc = cz.shape[0] // batch
    tri, eye = consts
    base = (2 * HEADS * DK + 2 * HEADS * DV)
    bq, bv = base // DK, (base + 2 * HEADS * DK) // DV
    lat = lambda cols, off: pl.BlockSpec((t, cols), lambda b, h, off=off: (b, off + h))
    ctx = lambda cols, off: pl.BlockSpec((tc, cols), lambda b, h, off=off: (b, off + h))
    const = lambda shape: pl.BlockSpec(shape, lambda b, h: (0,) * len(shape))
    in_specs = [lat(DK, bq), lat(DK, bq + HEADS), lat(DV, bv), lat(DV, bv + HEADS),
                pl.BlockSpec((None, t // CHUNK, 2 * N_DIR, CHUNK), lambda b, h: (h, b, 0, 0)),
                ctx(DK, bq), ctx(DK, bq + HEADS), ctx(DV, bv), ctx(DV, bv + HEADS),
                pl.BlockSpec((None, tc // CHUNK, 2 * N_DIR, CHUNK), lambda b, h: (h, b, 0, 0)),
                pl.BlockSpec((None, 2 * N_DIR, 1), lambda b, h: (h, 0, 0)),
                const((1, DV)), const(tri.shape), const(eye.shape)]
    out_specs = [pl.BlockSpec((t, DV), lambda b, h: (b, h))]
    out_shape = [jax.ShapeDtypeStruct((batch * t, HEADS * DV), BF16)]
    scratch = [pltpu.VMEM((N_DIR, DV, DK), F32), pltpu.VMEM((N_DIR, 1, DK), F32),
               pltpu.VMEM((N_DIR, 1, DK), F32), pltpu.VMEM((t, DV), F32)]
    if with_ctx_out:
        out_specs.append(pl.BlockSpec((tc, DV), lambda b, h: (b, h)))
        out_shape.append(jax.ShapeDtypeStruct((batch * tc, HEADS * DV), BF16))
        scratch.append(pltpu.VMEM((tc, DV), F32))
    bias = jnp.transpose(b_gate.reshape(2 * N_DIR, HEADS))[:, :, None]
    return pl.pallas_call(
        functools.partial(_mlstm_kernel, with_ctx_out=with_ctx_out),
        grid=(batch, HEADS),
        in_specs=in_specs,
        out_specs=out_specs,
        out_shape=out_shape,
        scratch_shapes=scratch,
        compiler_params=_params("arbitrary", "arbitrary"),
        name="mlstm_scan",
    )(z, z, z, z, gr, cz, cz, cz, cz, cgr, bias, g_norm, tri, eye)


def _prep_layer_weights(w_in, w_out, w_up, conv_w, conv_b, w_down):
    hk, hv = HEADS * DK, HEADS * DV
    glr0 = 2 * hk + 2 * hv
    ml0 = glr0 + GLA_LR
    gate0 = ml0 + 2 * hk + 2 * hv
    w_main = jnp.concatenate([w_in[:, :glr0], w_in[:, ml0:gate0]], axis=1).astype(BF16)
    w_small = jnp.concatenate([w_in[:, glr0:ml0], w_in[:, gate0:]], axis=1).astype(BF16)
    pad = D_FF_PAD - D_FF
    zc = jnp.zeros((D_MODEL, pad), w_up.dtype)
    w_up_p = jnp.concatenate([w_up[:, :D_FF], zc, w_up[:, D_FF:], zc], axis=1).astype(BF16)
    conv_w9 = jnp.pad(conv_w.reshape(9, D_FF), ((0, 0), (0, pad)))
    conv_bp = jnp.pad(conv_b.reshape(1, D_FF), ((0, 0), (0, pad)))
    w_down_p = jnp.pad(w_down, ((0, pad), (0, 0))).astype(BF16)
    return w_main, w_small, w_out.astype(BF16), w_up_p, conv_w9, conv_bp, w_down_p


def kernel(x, c, ctx, c_ctx, w_mod, b_mod, g_norm1, g_norm2, w_in, gla_w_lr, gla_b_lr, mlstm_b_gate,
           gla_g_norm, mlstm_g_norm, w_out, w_up, conv_w, conv_b, w_down, g_final):
    batch, seq, _ = x.shape
    ctx_len = ctx.shape[1]
    assert batch < MOD_ROWS and seq == GRID_W * GRID_W and ctx_len % CHUNK == 0
    tm_lat = 512
    tm_ctx = min(512, batch * ctx_len)
    cc = jnp.zeros((MOD_ROWS, D_MODEL), F32).at[:batch].set(c).at[batch].set(c_ctx)
    mod = _modulation(cc, w_mod, b_mod)
    gla_consts = _gla_constants()
    mlstm_consts = _mlstm_constants()

    xl = x.reshape(batch * seq, D_MODEL)
    xc = ctx.reshape(batch * ctx_len, D_MODEL)
    row = lambda v: v.reshape(1, -1)
    hl = _prenorm(xl, row(g_norm1[0]), mod, 0, seq, None, tm_lat)
    hc = _prenorm(xc, row(g_norm1[0]), mod, 0, tm_ctx, batch, tm_ctx)
    out = None
    for l in range(DEPTH):
        last = l == DEPTH - 1
        w_main, w_small, w_out_b, w_up_p, conv_w9, conv_bp, w_down_p = _prep_layer_weights(
            w_in[l], w_out[l], w_up[l], conv_w[l], conv_b[l], w_down[l])
        z, zs = _inproj(hl, w_main, w_small, 1024, 1024)
        cz, czs = _inproj(hc, w_main, w_small, tm_ctx, 1024)
        gla = _gla_scan(z, zs, cz, czs, gla_w_lr[l], gla_b_lr[l], row(gla_g_norm[l]), gla_consts, batch, not last)
        ml = _mlstm_scan(z, _gate_rows(zs, batch), cz, _gate_rows(czs, batch), mlstm_b_gate[l],
                         row(mlstm_g_norm[l]), mlstm_consts, batch, not last)
        x1, h2 = _outproj(gla[0], ml[0], w_out_b, xl, row(g_norm2[l]), mod, l, seq, None, 256)
        u = _upproj(h2, w_up_p, 1024, 1024)
        if last:
            out = _ffn(u, conv_w9, conv_bp, w_down_p, x1, mod, l, row(g_final), seq, None, GRID_W, True, True, tm_lat)
        else:
            xl, hl = _ffn(u, conv_w9, conv_bp, w_down_p, x1, mod, l, row(g_norm1[l + 1]), seq, None,
                          GRID_W, True, False, tm_lat)
            c1, hc2 = _outproj(gla[1], ml[1], w_out_b, xc, row(g_norm2[l]), mod, l, 256, batch, 256)
            cu = _upproj(hc2, w_up_p, tm_ctx, 1024)
            xc, hc = _ffn(cu, conv_w9, conv_bp, w_down_p, c1, mod, l, row(g_norm1[l + 1]), tm_ctx, batch,
                          ctx_len, False, False, tm_ctx)
    return out.reshape(batch, seq, D_MODEL)
```

```python
import functools

import numpy as np
import jax
import jax.numpy as jnp
from jax import lax
from jax.experimental import pallas as pl
from jax.experimental.pallas import tpu as pltpu

D_MODEL = 2048
DEPTH = 2
GRID_W = 64
N_DIR = 2
HEADS = 4
DK = 128
DV = 256
GLA_LR = 16
GLA_NORMALIZER = 16.0
CHUNK = 64
D_FF = 5504
D_FF_PAD = 5632
FF_TILE = 512
EPS = 1e-6
LANE = 128
MXU_N = 256
N_LEVELS = 6
N_MOD = 6
MOD_ROWS = 8

VMEM_LIMIT = 56 * 1024 * 1024

F32 = jnp.float32
BF16 = jnp.bfloat16
NN = (((1,), (0,)), ((), ()))
NT = (((1,), (1,)), ((), ()))
TN = (((0,), (0,)), ((), ()))


def _mm(a, b, dims=NN):
    return lax.dot_general(a, b, dims, preferred_element_type=F32)


def _split(x):
    hi = x.astype(BF16)
    return hi, (x - hi.astype(F32)).astype(BF16)


def _mm3(a, b):
    ah, al = _split(a)
    bh, bl = _split(b)
    return _mm(ah, bh) + (_mm(ah, bl) + _mm(al, bh))


def _sigmoid(x):
    return 1.0 / (1.0 + jnp.exp(-x))


def _log_sigmoid(x):
    return jnp.minimum(x, 0.0) - jnp.log(1.0 + jnp.exp(-jnp.abs(x)))


def _rms(x, g):
    return x * lax.rsqrt(jnp.mean(x * x, axis=-1, keepdims=True) + EPS) * g


def _params(*sem):
    return pltpu.CompilerParams(dimension_semantics=sem, vmem_limit_bytes=VMEM_LIMIT)


def _mod_kernel(cc_ref, w_ref, b_ref, o_ref):
    a = cc_ref[...]
    a = a * _sigmoid(a)
    o_ref[...] = _mm3(a, w_ref[...]) + b_ref[...]


def _modulation(cc, w_mod, b_mod):
    tn = 1024
    out = pl.pallas_call(
        _mod_kernel,
        grid=(DEPTH, N_MOD * D_MODEL // tn),
        in_specs=[pl.BlockSpec((MOD_ROWS, D_MODEL), lambda l, j: (0, 0)),
                  pl.BlockSpec((None, D_MODEL, tn), lambda l, j: (l, 0, j)),
                  pl.BlockSpec((None, 1, tn), lambda l, j: (l, 0, j))],
        out_specs=pl.BlockSpec((None, MOD_ROWS, tn), lambda l, j: (l, 0, j)),
        out_shape=jax.ShapeDtypeStruct((DEPTH, MOD_ROWS, N_MOD * D_MODEL), F32),
        compiler_params=_params("arbitrary", "arbitrary"),
        name="modulation",
    )(cc, w_mod, b_mod.reshape(DEPTH, 1, N_MOD * D_MODEL))
    return out.reshape(DEPTH * MOD_ROWS * N_MOD, 1, D_MODEL)


def _mod_spec(layer, piece, seq, ctx_row, tm):
    base = layer * MOD_ROWS * N_MOD + piece
    assert seq % tm == 0
    blocks_per_batch = seq // tm

    def idx(i, *_):
        r = ctx_row if ctx_row is not None else i // blocks_per_batch
        return (base + r * N_MOD, 0, 0)

    return pl.BlockSpec((None, 1, D_MODEL), idx)


def _prenorm_kernel(x_ref, g_ref, sc_ref, sh_ref, o_ref):
    o_ref[...] = (_rms(x_ref[...], g_ref[...]) * (1.0 + sc_ref[...]) + sh_ref[...]).astype(o_ref.dtype)


def _prenorm(x, g, mod, layer, seq, ctx_row, tm):
    n = x.shape[0]
    return pl.pallas_call(
        _prenorm_kernel,
        grid=(n // tm,),
        in_specs=[pl.BlockSpec((tm, D_MODEL), lambda i: (i, 0)),
                  pl.BlockSpec((1, D_MODEL), lambda i: (0, 0)),
                  _mod_spec(layer, 1, seq, ctx_row, tm),
                  _mod_spec(layer, 0, seq, ctx_row, tm)],
        out_specs=pl.BlockSpec((tm, D_MODEL), lambda i: (i, 0)),
        out_shape=jax.ShapeDtypeStruct((n, D_MODEL), BF16),
        compiler_params=_params("arbitrary"),
        name="prenorm",
    )(x, g, mod, mod)


def _inproj_kernel(h_ref, w_ref, ws_ref, z_ref, zs_ref):
    h = h_ref[...]
    z_ref[...] = _mm(h, w_ref[...]).astype(z_ref.dtype)

    @pl.when(pl.program_id(1) == 0)
    def _():
        zs_ref[...] = _mm(h, ws_ref[...])


def _inproj(h, w_main, w_small, tm, tn):
    n = h.shape[0]
    p = w_main.shape[1]
    ps = w_small.shape[1]
    return pl.pallas_call(
        _inproj_kernel,
        grid=(n // tm, p // tn),
        in_specs=[pl.BlockSpec((tm, D_MODEL), lambda i, j: (i, 0)),
                  pl.BlockSpec((D_MODEL, tn), lambda i, j: (0, j)),
                  pl.BlockSpec((D_MODEL, ps), lambda i, j: (0, 0))],
        out_specs=[pl.BlockSpec((tm, tn), lambda i, j: (i, j)),
                   pl.BlockSpec((tm, ps), lambda i, j: (i, 0))],
        out_shape=[jax.ShapeDtypeStruct((n, p), BF16), jax.ShapeDtypeStruct((n, ps), F32)],
        compiler_params=_params("arbitrary", "arbitrary"),
        name="inproj",
    )(h, w_main, w_small)


def _matmul_kernel(h_ref, w_ref, z_ref):
    z_ref[...] = _mm(h_ref[...], w_ref[...]).astype(z_ref.dtype)


def _upproj(h, w, tm, tn):
    n = h.shape[0]
    p = w.shape[1]
    return pl.pallas_call(
        _matmul_kernel,
        grid=(n // tm, p // tn),
        in_specs=[pl.BlockSpec((tm, D_MODEL), lambda i, j: (i, 0)),
                  pl.BlockSpec((D_MODEL, tn), lambda i, j: (0, j))],
        out_specs=pl.BlockSpec((tm, tn), lambda i, j: (i, j)),
        out_shape=jax.ShapeDtypeStruct((n, p), BF16),
        compiler_params=_params("arbitrary", "arbitrary"),
        name="upproj",
    )(h, w)


def _outproj_kernel(yg_ref, ym_ref, wg_ref, wm_ref, x_ref, gt_ref, g2_ref, sc_ref, sh_ref, x1_ref, h2_ref):
    acc = _mm(yg_ref[...], wg_ref[...]) + _mm(ym_ref[...], wm_ref[...])
    x1 = x_ref[...] + gt_ref[...] * acc
    x1_ref[...] = x1
    h2_ref[...] = (_rms(x1, g2_ref[...]) * (1.0 + sc_ref[...]) + sh_ref[...]).astype(h2_ref.dtype)


def _outproj(yg, ym, w_out, x, g2, mod, layer, seq, ctx_row, tm):
    n = x.shape[0]
    half = w_out.shape[0] // 2
    row = lambda i: (i, 0)
    return pl.pallas_call(
        _outproj_kernel,
        grid=(n // tm,),
        in_specs=[pl.BlockSpec((tm, half), row),
                  pl.BlockSpec((tm, half), row),
                  pl.BlockSpec((half, D_MODEL), lambda i: (0, 0)),
                  pl.BlockSpec((half, D_MODEL), lambda i: (1, 0)),
                  pl.BlockSpec((tm, D_MODEL), row),
                  _mod_spec(layer, 2, seq, ctx_row, tm),
                  pl.BlockSpec((1, D_MODEL), lambda i: (0, 0)),
                  _mod_spec(layer, 4, seq, ctx_row, tm),
                  _mod_spec(layer, 3, seq, ctx_row, tm)],
        out_specs=[pl.BlockSpec((tm, D_MODEL), row), pl.BlockSpec((tm, D_MODEL), row)],
        out_shape=[jax.ShapeDtypeStruct((n, D_MODEL), F32), jax.ShapeDtypeStruct((n, D_MODEL), BF16)],
        compiler_params=_params("arbitrary"),
        name="outproj",
    )(yg, ym, w_out, w_out, x, mod, g2, mod, mod)


def _ffn_kernel(*refs, grid_w, vertical, final, tm, tf):
    if vertical:
        ug_ref, top_ref, bot_ref = refs[:3]
        refs = refs[3:]
    else:
        ug_ref = refs[0]
        refs = refs[1:]
    uv_ref, cw_ref, cb_ref, wd_ref, x1_ref, gt_ref, gn_ref = refs[:7]
    refs = refs[7:]
    if final:
        out_ref, acc_ref, act_a_ref, act_b_ref = refs
    else:
        sc_ref, sh_ref, x2_ref, hn_ref, acc_ref, act_a_ref, act_b_ref = refs
    i = pl.program_id(0)
    f = pl.program_id(1)
    last = pl.num_programs(1) - 1

    @pl.when(jnp.logical_and(i == 0, f == 0))
    def _():
        act_b_ref[...] = jnp.zeros_like(act_b_ref)

    n_rows = tm // grid_w
    if vertical:
        blocks_per_batch = (grid_w * grid_w) // tm
        top_ok = (i % blocks_per_batch != 0).astype(F32)
        bot_ok = (i % blocks_per_batch != blocks_per_batch - 1).astype(F32)

    def stage(cur_ref, prev_ref):
        col = lax.broadcasted_iota(jnp.int32, (grid_w, LANE), 0)
        n_slabs = (tf // LANE) * n_rows
        n_pieces = min(D_MODEL // MXU_N, n_slabs)
        piece = D_MODEL // n_pieces
        done = 0
        for c in range(tf // LANE):
            lanes = slice(c * LANE, (c + 1) * LANE)
            cw = cw_ref[:, lanes]
            cb = cb_ref[:, lanes]
            for r in range(n_rows):
                rows = slice(r * grid_w, (r + 1) * grid_w)
                if done * n_pieces % n_slabs == 0:
                    j = done * n_pieces // n_slabs
                    cols = slice(j * piece, (j + 1) * piece)
                    acc_ref[:, cols] += _mm(prev_ref[...], wd_ref[:, cols])
                done += 1
                mid = ug_ref[rows, lanes].astype(F32)
                if vertical:
                    if r > 0:
                        up = ug_ref[(r - 1) * grid_w:r * grid_w, lanes].astype(F32)
                    else:
                        up = top_ref[:, lanes].astype(F32) * top_ok
                    if r < n_rows - 1:
                        down = ug_ref[(r + 1) * grid_w:(r + 2) * grid_w, lanes].astype(F32)
                    else:
                        down = bot_ref[:, lanes].astype(F32) * bot_ok
                    taps = [cw[k:k + 1] * up + cw[3 + k:4 + k] * mid + cw[6 + k:7 + k] * down for k in range(3)]
                else:
                    taps = [cw[3 + k:4 + k] * mid for k in range(3)]
                left = jnp.where(col != 0, pltpu.roll(taps[0], 1, 0), 0.0)
                right = jnp.where(col != grid_w - 1, pltpu.roll(taps[2], grid_w - 1, 0), 0.0)
                g = taps[1] + left + right + cb
                half_g = 0.5 * g
                silu = half_g * (1.0 + jnp.tanh(half_g))
                cur_ref[rows, lanes] = (silu * uv_ref[rows, lanes].astype(F32)).astype(BF16)

    @pl.when(f % 2 == 0)
    def _():
        stage(act_a_ref, act_b_ref)

    @pl.when(f % 2 == 1)
    def _():
        stage(act_b_ref, act_a_ref)

    @pl.when(f == 0)
    def _():
        acc_ref[...] = jnp.zeros_like(acc_ref)

    @pl.when(f == last)
    def _():
        x2 = x1_ref[...] + gt_ref[...] * acc_ref[...]
        if final:
            out_ref[...] = _rms(x2, gn_ref[...])
        else:
            x2_ref[...] = x2
            hn_ref[...] = (_rms(x2, gn_ref[...]) * (1.0 + sc_ref[...]) + sh_ref[...]).astype(hn_ref.dtype)


def _ffn(u, conv_w9, conv_b, w_down, x1, mod, layer, gn, seq, ctx_row, grid_w, vertical, final, tm):
    n = x1.shape[0]
    tf = FF_TILE
    nf = D_FF_PAD // tf
    assert nf % 2 == 1
    halo = grid_w
    hb = tm // halo
    n_halo_blocks = n // halo
    row = lambda i, f: (i, 0)
    conv_f = lambda f: jnp.minimum(f, nf - 1)
    in_specs = [pl.BlockSpec((tm, tf), lambda i, f: (i, conv_f(f)))]
    args = [u]
    if vertical:
        in_specs += [pl.BlockSpec((halo, tf), lambda i, f: (jnp.maximum(i * hb - 1, 0), conv_f(f))),
                     pl.BlockSpec((halo, tf), lambda i, f: (jnp.minimum(i * hb + hb, n_halo_blocks - 1), conv_f(f)))]
        args += [u, u]
    in_specs += [pl.BlockSpec((tm, tf), lambda i, f: (i, nf + conv_f(f))),
                 pl.BlockSpec((9, tf), lambda i, f: (0, conv_f(f))),
                 pl.BlockSpec((1, tf), lambda i, f: (0, conv_f(f))),
                 pl.BlockSpec((tf, D_MODEL), lambda i, f: (jnp.maximum(f - 1, 0), 0)),
                 pl.BlockSpec((tm, D_MODEL), row),
                 _mod_spec(layer, 5, seq, ctx_row, tm),
                 pl.BlockSpec((1, D_MODEL), lambda i, f: (0, 0))]
    args += [u, conv_w9, conv_b, w_down, x1, mod, gn]
    if final:
        out_specs = pl.BlockSpec((tm, D_MODEL), row)
        out_shape = jax.ShapeDtypeStruct((n, D_MODEL), F32)
    else:
        in_specs += [_mod_spec(layer + 1, 1, seq, ctx_row, tm),
                     _mod_spec(layer + 1, 0, seq, ctx_row, tm)]
        args += [mod, mod]
        out_specs = [pl.BlockSpec((tm, D_MODEL), row), pl.BlockSpec((tm, D_MODEL), row)]
        out_shape = [jax.ShapeDtypeStruct((n, D_MODEL), F32), jax.ShapeDtypeStruct((n, D_MODEL), BF16)]
    return pl.pallas_call(
        functools.partial(_ffn_kernel, grid_w=grid_w, vertical=vertical, final=final, tm=tm, tf=tf),
        grid=(n // tm, nf + 1),
        in_specs=in_specs,
        out_specs=out_specs,
        out_shape=out_shape,
        scratch_shapes=[pltpu.VMEM((tm, D_MODEL), F32), pltpu.VMEM((tm, tf), BF16), pltpu.VMEM((tm, tf), BF16)],
        compiler_params=_params("arbitrary", "arbitrary"),
        name="ffn",
    )(*args)


def _flip2(a):
    return a[..., ::-1, ::-1]


def _gla_constants():
    c = CHUNK
    m = np.zeros((N_LEVELS + 2, c, c), np.float32)
    later = np.zeros((N_LEVELS, c, 1), np.float32)
    pair = np.zeros((N_LEVELS + 1, c, c), np.float32)
    for lvl in range(N_LEVELS):
        n = c >> lvl
        half = n // 2
        for r in range(c):
            bnd = (r // n) * n + half - 1
            if r % n >= half:
                m[lvl, r, bnd + 1:r + 1] = 1.0
                later[lvl, r] = 1.0
            else:
                m[lvl, r, r + 1:bnd + 1] = 1.0
        blk = np.arange(c) // n
        pos = np.arange(c) % n
        pair[lvl] = (blk[:, None] == blk[None, :]) & (pos[:, None] >= half) & (pos[None, :] < half)
    pair[N_LEVELS] = np.eye(c)
    for r in range(c):
        m[N_LEVELS, r, :r + 1] = 1.0
        m[N_LEVELS + 1, r, r + 1:] = 1.0
    m = np.stack([m, _flip2(m)]).reshape(N_DIR, (N_LEVELS + 2) * c, c)
    later = np.stack([later, later[:, ::-1]])
    later = np.broadcast_to(later, (N_DIR, N_LEVELS, c, DK))
    pair = np.stack([pair, _flip2(pair)])
    return jnp.asarray(m, BF16), jnp.asarray(later, F32), jnp.asarray(pair, F32)


def _mlstm_constants():
    tri = np.tril(np.ones((CHUNK, CHUNK), np.float32))
    return jnp.asarray(np.stack([tri, tri.T]), F32), jnp.asarray(np.eye(CHUNK), F32)


def _drive_scan(n_ctx, n_lat, with_ctx_out, streams):
    assert n_ctx % 2 == 0 and n_lat % 2 == 0

    def one(stream, d, is_ctx, s, n_total, second_half, want_out):
        chunk, finish, lat_io, ctx_io = stream
        gate_ref, acc_ref, y_ref = ctx_io if is_ctx else lat_io
        n = s if d == 0 else n_total - 1 - s
        start = n * CHUNK if isinstance(n, int) else pl.multiple_of(n * CHUNK, CHUNK)
        rows = pl.ds(start, CHUNK)
        o = yield from chunk(d, is_ctx, n, rows, want_out)
        if not want_out:
            return
        if second_half:
            y_ref[rows, :] = finish(acc_ref[rows, :] + o, gate_ref[rows, :])
        else:
            acc_ref[rows, :] = o

    def step(*args):
        live = [one(stream, d, *args) for d in range(N_DIR) for stream in streams]
        while live:
            for g in list(live):
                try:
                    next(g)
                except StopIteration:
                    live.remove(g)

    for s in range(n_ctx):
        step(True, s, n_ctx, s >= n_ctx // 2, with_ctx_out)
    for half in range(2):
        def body(s, carry, half=half):
            step(False, s, n_lat, half == 1, True)
            return carry

        lax.fori_loop(half * (n_lat // 2), (half + 1) * (n_lat // 2), body, 0)


def _gla_chunk(d, qc, kc, vc, glr, wlr, blr, seg_ref, later_ref, pair_ref, st_ref, want_out):
    c = CHUNK
    la = _log_sigmoid(_mm3(glr, wlr) + blr) * (1.0 / GLA_NORMALIZER)
    yield
    la_hi, la_lo = _split(la)
    seg = seg_ref[d]
    e_all = jnp.exp(_mm(seg, la_hi) + _mm(seg, la_lo))
    yield
    e_cum = e_all[N_LEVELS * c:(N_LEVELS + 1) * c]
    e_end = e_all[(N_LEVELS + 1) * c:(N_LEVELS + 2) * c]
    last = c - 1 if d == 0 else 0
    decay = e_cum[last:last + 1]
    qf = qc.astype(F32) * (DK ** -0.5)
    kf = kc.astype(F32)
    st = st_ref[d]
    out = None
    if want_out:
        a = pair_ref[d, N_LEVELS] * _mm(qf.astype(BF16), kc, NT)
        for lvl in range(N_LEVELS):
            e = e_all[lvl * c:(lvl + 1) * c]
            later = later_ref[d, lvl]
            ql = (qf * e * later).astype(BF16)
            kl = (kf * e * (1.0 - later)).astype(BF16)
            a = a + pair_ref[d, lvl] * _mm(ql, kl, NT)
            yield
        out = _mm(a.astype(BF16), vc) + _mm((qf * e_cum).astype(BF16), st.astype(BF16), NT)
    st_ref[d] = st * decay + _mm(vc, (kf * e_end).astype(BF16), TN)
    return out


N_GLA_IN = 16


def _gla_stream(in_refs, y_ref, cy_ref, st_ref, o_ref, co_ref):
    (q_ref, k_ref, v_ref, g_ref, zs_ref, cq_ref, ck_ref, cv_ref, cg_ref, czs_ref,
     wlr_ref, blr_ref, gn_ref, seg_ref, later_ref, pair_ref) = in_refs

    def finish(tot, gate):
        gate = gate.astype(F32)
        return (_rms(tot, gn_ref[...]) * (gate * _sigmoid(gate))).astype(BF16)

    def chunk(d, is_ctx, n, rows, want_out):
        del n
        qr, kr, vr, zr = (cq_ref, ck_ref, cv_ref, czs_ref) if is_ctx else (q_ref, k_ref, v_ref, zs_ref)
        return (yield from _gla_chunk(d, qr[rows, :], kr[rows, :], vr[rows, :], zr[rows, 0:GLA_LR], wlr_ref[d],
                                      blr_ref[d], seg_ref, later_ref, pair_ref, st_ref, want_out))

    st_ref[...] = jnp.zeros_like(st_ref)
    return chunk, finish, (g_ref, o_ref, y_ref), (cg_ref, co_ref, cy_ref)


def _gla_specs(t, tc, zs_cols, consts):
    seg, later, pair = consts
    lat = lambda cols, off: pl.BlockSpec((t, cols), lambda b, h, off=off: (b, off + h))
    ctx = lambda cols, off: pl.BlockSpec((tc, cols), lambda b, h, off=off: (b, off + h))
    const = lambda shape: pl.BlockSpec(shape, lambda b, h: (0,) * len(shape))
    return [lat(DK, 0), lat(DK, HEADS), lat(DV, HEADS), lat(DV, 2 * HEADS),
            pl.BlockSpec((t, zs_cols), lambda b, h: (b, 0)),
            ctx(DK, 0), ctx(DK, HEADS), ctx(DV, HEADS), ctx(DV, 2 * HEADS),
            pl.BlockSpec((tc, zs_cols), lambda b, h: (b, 0)),
            pl.BlockSpec((N_DIR, GLA_LR, DK), lambda b, h: (0, 0, h)),
            pl.BlockSpec((N_DIR, 1, DK), lambda b, h: (0, 0, h)),
            const((1, DV)), const(seg.shape), const(later.shape), const(pair.shape)]


def _mlstm_chunk(d, qc, kc, vc, gates, tri, tri_t, eye, ct_ref, n_ref, m_ref, want_out):
    li_row = gates[2 * d:2 * d + 1]
    lf_row = _log_sigmoid(gates[2 * d + 1:2 * d + 2])
    li_col = jnp.sum(eye * li_row, axis=1, keepdims=True)
    lf_col = jnp.sum(eye * lf_row, axis=1, keepdims=True)
    fcum_col = jnp.sum(tri * lf_row, axis=1, keepdims=True)
    fcum_row = jnp.sum(tri_t * lf_col, axis=0, keepdims=True)
    f_end = jnp.sum(lf_row, axis=1, keepdims=True)
    yield
    m = m_ref[d][:, 0:1]
    m_new = jnp.maximum(f_end + m, jnp.max(f_end - fcum_row + li_row, axis=1, keepdims=True))
    carry = jnp.exp(f_end + m - m_new)
    w_col = jnp.exp(f_end - fcum_col + li_col - m_new)
    kf = kc.astype(F32) * (DK ** -0.5)
    kw = kf * w_col
    yield
    ct = ct_ref[d]
    nvec = n_ref[d]
    out = None
    if want_out:
        qf = qc.astype(F32)
        d_log = fcum_col - fcum_row + li_row
        causal = tri > 0.5
        inter_log = fcum_col + m
        m_q = jnp.maximum(inter_log, jnp.max(jnp.where(causal, d_log, -jnp.inf), axis=1, keepdims=True))
        p = jnp.where(causal, jnp.exp(d_log - m_q), 0.0)
        s = _mm(qc, kf.astype(BF16), NT) * p
        inter = jnp.exp(inter_log - m_q)
        yield
        num = _mm(s.astype(BF16), vc) + inter * _mm(qc, ct.astype(BF16), NT)
        den = jnp.sum(s, axis=1, keepdims=True) + inter * jnp.sum(qf * nvec, axis=1, keepdims=True)
        out = num / jnp.maximum(jnp.abs(den), jnp.exp(-m_q))
    ct_ref[d] = carry * ct + _mm(vc, kw.astype(BF16), TN)
    n_ref[d] = carry * nvec + jnp.sum(kw, axis=0, keepdims=True)
    m_ref[d] = jnp.broadcast_to(m_new, nvec.shape)
    return out


N_MLSTM_IN = 14


def _mlstm_stream(in_refs, y_ref, cy_ref, ct_ref, n_ref, m_ref, o_ref, co_ref):
    (q_ref, k_ref, v_ref, g_ref, gr_ref, cq_ref, ck_ref, cv_ref, cg_ref, cgr_ref,
     bias_ref, gn_ref, tri_ref, eye_ref) = in_refs

    def finish(tot, gate):
        return (_sigmoid(gate.astype(F32)) * _rms(tot, gn_ref[...])).astype(BF16)

    def chunk(d, is_ctx, n, rows, want_out):
        qr, kr, vr, grr = (cq_ref, ck_ref, cv_ref, cgr_ref) if is_ctx else (q_ref, k_ref, v_ref, gr_ref)
        return (yield from _mlstm_chunk(d, qr[rows, :], kr[rows, :], vr[rows, :], grr[n] + bias_ref[...],
                                        tri_ref[d], tri_ref[1 - d], eye_ref[...], ct_ref, n_ref, m_ref, want_out))

    ct_ref[...] = jnp.zeros_like(ct_ref)
    n_ref[...] = jnp.zeros_like(n_ref)
    m_ref[...] = jnp.zeros_like(m_ref)
    return chunk, finish, (g_ref, o_ref, y_ref), (cg_ref, co_ref, cy_ref)


def _gate_rows(zs):
    n = zs.shape[0]
    g = zs[:, GLA_LR:].reshape(n // CHUNK, CHUNK, N_DIR * 2, HEADS)
    return jnp.transpose(g, (3, 0, 2, 1))


def _mlstm_specs(t, tc, consts):
    tri, eye = consts
    base = (2 * HEADS * DK + 2 * HEADS * DV)
    bq, bv = base // DK, (base + 2 * HEADS * DK) // DV
    lat = lambda cols, off: pl.BlockSpec((t, cols), lambda b, h, off=off: (b, off + h))
    ctx = lambda cols, off: pl.BlockSpec((tc, cols), lambda b, h, off=off: (b, off + h))
    const = lambda shape: pl.BlockSpec(shape, lambda b, h: (0,) * len(shape))
    return [lat(DK, bq), lat(DK, bq + HEADS), lat(DV, bv), lat(DV, bv + HEADS),
            pl.BlockSpec((None, t // CHUNK, 2 * N_DIR, CHUNK), lambda b, h: (h, b, 0, 0)),
            ctx(DK, bq), ctx(DK, bq + HEADS), ctx(DV, bv), ctx(DV, bv + HEADS),
            pl.BlockSpec((None, tc // CHUNK, 2 * N_DIR, CHUNK), lambda b, h: (h, b, 0, 0)),
            pl.BlockSpec((None, 2 * N_DIR, 1), lambda b, h: (h, 0, 0)),
            const((1, DV)), const(tri.shape), const(eye.shape)]


def _scan_kernel(*refs, with_ctx_out):
    gla_in = refs[:N_GLA_IN]
    ml_in = refs[N_GLA_IN:N_GLA_IN + N_MLSTM_IN]
    rest = refs[N_GLA_IN + N_MLSTM_IN:]
    if with_ctx_out:
        yg_ref, ym_ref, cyg_ref, cym_ref, st_ref, ct_ref, n_ref, m_ref, og_ref, om_ref, cog_ref, com_ref = rest
    else:
        yg_ref, ym_ref, st_ref, ct_ref, n_ref, m_ref, og_ref, om_ref = rest
        cyg_ref = cym_ref = cog_ref = com_ref = None
    streams = [_gla_stream(gla_in, yg_ref, cyg_ref, st_ref, og_ref, cog_ref),
               _mlstm_stream(ml_in, ym_ref, cym_ref, ct_ref, n_ref, m_ref, om_ref, com_ref)]
    _drive_scan(gla_in[5].shape[0] // CHUNK, gla_in[0].shape[0] // CHUNK, with_ctx_out, streams)


def _scans(z, zs, cz, czs, w_lr, b_lr, gla_g_norm, b_gate, mlstm_g_norm, gla_consts, mlstm_consts, batch,
           with_ctx_out):
    t = z.shape[0] // batch
    tc = cz.shape[0] // batch
    in_specs = _gla_specs(t, tc, zs.shape[1], gla_consts) + _mlstm_specs(t, tc, mlstm_consts)
    head_out = lambda rows: pl.BlockSpec((rows, DV), lambda b, h: (b, h))
    out_specs = [head_out(t), head_out(t)]
    out_shape = [jax.ShapeDtypeStruct((batch * t, HEADS * DV), BF16)] * 2
    state = [pltpu.VMEM((N_DIR, DV, DK), F32), pltpu.VMEM((N_DIR, DV, DK), F32),
             pltpu.VMEM((N_DIR, 1, DK), F32), pltpu.VMEM((N_DIR, 1, DK), F32)]
    scratch = state + [pltpu.VMEM((t, DV), F32)] * 2
    if with_ctx_out:
        out_specs += [head_out(tc), head_out(tc)]
        out_shape += [jax.ShapeDtypeStruct((batch * tc, HEADS * DV), BF16)] * 2
        scratch += [pltpu.VMEM((tc, DV), F32)] * 2
    bias = jnp.transpose(b_gate.reshape(2 * N_DIR, HEADS))[:, :, None]
    return pl.pallas_call(
        functools.partial(_scan_kernel, with_ctx_out=with_ctx_out),
        grid=(batch, HEADS),
        in_specs=in_specs,
        out_specs=out_specs,
        out_shape=out_shape,
        scratch_shapes=scratch,
        compiler_params=_params("arbitrary", "arbitrary"),
        name="scans",
    )(z, z, z, z, zs, cz, cz, cz, cz, czs, w_lr, b_lr.reshape(N_DIR, 1, HEADS * DK), gla_g_norm,
      *gla_consts,
      z, z, z, z, _gate_rows(zs), cz, cz, cz, cz, _gate_rows(czs), bias, mlstm_g_norm, *mlstm_consts)


def _prep_layer_weights(w_in, w_out, w_up, conv_w, conv_b, w_down):
    hk, hv = HEADS * DK, HEADS * DV
    glr0 = 2 * hk + 2 * hv
    ml0 = glr0 + GLA_LR
    gate0 = ml0 + 2 * hk + 2 * hv
    w_main = jnp.concatenate([w_in[:, :glr0], w_in[:, ml0:gate0]], axis=1).astype(BF16)
    w_small = jnp.concatenate([w_in[:, glr0:ml0], w_in[:, gate0:]], axis=1).astype(BF16)
    pad = D_FF_PAD - D_FF
    zc = jnp.zeros((D_MODEL, pad), w_up.dtype)
    w_up_p = jnp.concatenate([w_up[:, :D_FF], zc, w_up[:, D_FF:], zc], axis=1).astype(BF16)
    conv_w9 = jnp.pad(conv_w.reshape(9, D_FF), ((0, 0), (0, pad)))
    conv_bp = jnp.pad(conv_b.reshape(1, D_FF), ((0, 0), (0, pad)))
    w_down_p = jnp.pad(w_down, ((0, pad), (0, 0))).astype(BF16)
    return w_main, w_small, w_out.astype(BF16), w_up_p, conv_w9, conv_bp, w_down_p


def kernel(x, c, ctx, c_ctx, w_mod, b_mod, g_norm1, g_norm2, w_in, gla_w_lr, gla_b_lr, mlstm_b_gate,
           gla_g_norm, mlstm_g_norm, w_out, w_up, conv_w, conv_b, w_down, g_final):
    batch, seq, _ = x.shape
    ctx_len = ctx.shape[1]
    assert batch < MOD_ROWS and seq == GRID_W * GRID_W and ctx_len % CHUNK == 0
    tm_lat = 512
    tm_ctx = min(512, batch * ctx_len)
    cc = jnp.zeros((MOD_ROWS, D_MODEL), F32).at[:batch].set(c).at[batch].set(c_ctx)
    mod = _modulation(cc, w_mod, b_mod)
    gla_consts = _gla_constants()
    mlstm_consts = _mlstm_constants()

    xl = x.reshape(batch * seq, D_MODEL)
    xc = ctx.reshape(batch * ctx_len, D_MODEL)
    row = lambda v: v.reshape(1, -1)
    hl = _prenorm(xl, row(g_norm1[0]), mod, 0, seq, None, tm_lat)
    hc = _prenorm(xc, row(g_norm1[0]), mod, 0, tm_ctx, batch, tm_ctx)
    out = None
    for l in range(DEPTH):
        last = l == DEPTH - 1
        w_main, w_small, w_out_b, w_up_p, conv_w9, conv_bp, w_down_p = _prep_layer_weights(
            w_in[l], w_out[l], w_up[l], conv_w[l], conv_b[l], w_down[l])
        z, zs = _inproj(hl, w_main, w_small, 1024, 1024)
        cz, czs = _inproj(hc, w_main, w_small, tm_ctx, 1024)
        ys = _scans(z, zs, cz, czs, gla_w_lr[l], gla_b_lr[l], row(gla_g_norm[l]), mlstm_b_gate[l],
                    row(mlstm_g_norm[l]), gla_consts, mlstm_consts, batch, not last)
        x1, h2 = _outproj(ys[0], ys[1], w_out_b, xl, row(g_norm2[l]), mod, l, seq, None, 256)
        u = _upproj(h2, w_up_p, 1024, 1024)
        if last:
            out = _ffn(u, conv_w9, conv_bp, w_down_p, x1, mod, l, row(g_final), seq, None, GRID_W, True, True, tm_lat)
        else:
            xl, hl = _ffn(u, conv_w9, conv_bp, w_down_p, x1, mod, l, row(g_norm1[l + 1]), seq, None,
                          GRID_W, True, False, tm_lat)
            c1, hc2 = _outproj(ys[2], ys[3], w_out_b, xc, row(g_norm2[l]), mod, l, 256, batch, 256)
            cu = _upproj(hc2, w_up_p, tm_ctx, 1024)
            xc, hc = _ffn(cu, conv_w9, conv_bp, w_down_p, c1, mod, l, row(g_norm1[l + 1]), tm_ctx, batch,
                          ctx_len, False, False, tm_ctx)
    return out.reshape(batch, seq, D_MODEL)
```

```python
import functools

import numpy as np
import jax
import jax.numpy as jnp
from jax import lax
from jax.experimental import pallas as pl
from jax.experimental.pallas import tpu as pltpu

D_MODEL = 2048
DEPTH = 2
GRID_W = 64
N_DIR = 2
HEADS = 4
DK = 128
DV = 256
GLA_LR = 16
GLA_NORMALIZER = 16.0
CHUNK = 64
D_FF = 5504
D_FF_PAD = 5632
FF_TILE = 512
EPS = 1e-6
LANE = 128
SCAN_UNROLL = 4
N_LEVELS = 6
N_MOD = 6
MOD_ROWS = 8

VMEM_LIMIT = 56 * 1024 * 1024

F32 = jnp.float32
BF16 = jnp.bfloat16
NN = (((1,), (0,)), ((), ()))
NT = (((1,), (1,)), ((), ()))
TN = (((0,), (0,)), ((), ()))


def _mm(a, b, dims=NN):
    return lax.dot_general(a, b, dims, preferred_element_type=F32)


def _split(x):
    hi = x.astype(BF16)
    return hi, (x - hi.astype(F32)).astype(BF16)


def _mm3(a, b):
    ah, al = _split(a)
    bh, bl = _split(b)
    return _mm(ah, bh) + (_mm(ah, bl) + _mm(al, bh))


def _sigmoid(x):
    return 1.0 / (1.0 + jnp.exp(-x))


def _log_sigmoid(x):
    return jnp.minimum(x, 0.0) - jnp.log(1.0 + jnp.exp(-jnp.abs(x)))


def _rms(x, g):
    return x * lax.rsqrt(jnp.mean(x * x, axis=-1, keepdims=True) + EPS) * g


def _params(*sem):
    return pltpu.CompilerParams(dimension_semantics=sem, vmem_limit_bytes=VMEM_LIMIT)


def _mod_kernel(cc_ref, w_ref, b_ref, o_ref):
    a = cc_ref[...]
    a = a * _sigmoid(a)
    o_ref[...] = _mm3(a, w_ref[...]) + b_ref[...]


def _modulation(cc, w_mod, b_mod):
    tn = 1024
    out = pl.pallas_call(
        _mod_kernel,
        grid=(DEPTH, N_MOD * D_MODEL // tn),
        in_specs=[pl.BlockSpec((MOD_ROWS, D_MODEL), lambda l, j: (0, 0)),
                  pl.BlockSpec((None, D_MODEL, tn), lambda l, j: (l, 0, j)),
                  pl.BlockSpec((None, 1, tn), lambda l, j: (l, 0, j))],
        out_specs=pl.BlockSpec((None, MOD_ROWS, tn), lambda l, j: (l, 0, j)),
        out_shape=jax.ShapeDtypeStruct((DEPTH, MOD_ROWS, N_MOD * D_MODEL), F32),
        compiler_params=_params("arbitrary", "arbitrary"),
        name="modulation",
    )(cc, w_mod, b_mod.reshape(DEPTH, 1, N_MOD * D_MODEL))
    return out.reshape(DEPTH * MOD_ROWS * N_MOD, 1, D_MODEL)


def _mod_spec(layer, piece, seq, ctx_row, tm):
    base = layer * MOD_ROWS * N_MOD + piece
    assert seq % tm == 0
    blocks_per_batch = seq // tm

    def idx(i, *_):
        r = ctx_row if ctx_row is not None else i // blocks_per_batch
        return (base + r * N_MOD, 0, 0)

    return pl.BlockSpec((None, 1, D_MODEL), idx)


def _prenorm_kernel(x_ref, g_ref, sc_ref, sh_ref, o_ref):
    o_ref[...] = (_rms(x_ref[...], g_ref[...]) * (1.0 + sc_ref[...]) + sh_ref[...]).astype(o_ref.dtype)


def _prenorm(x, g, mod, layer, seq, ctx_row, tm):
    n = x.shape[0]
    return pl.pallas_call(
        _prenorm_kernel,
        grid=(n // tm,),
        in_specs=[pl.BlockSpec((tm, D_MODEL), lambda i: (i, 0)),
                  pl.BlockSpec((1, D_MODEL), lambda i: (0, 0)),
                  _mod_spec(layer, 1, seq, ctx_row, tm),
                  _mod_spec(layer, 0, seq, ctx_row, tm)],
        out_specs=pl.BlockSpec((tm, D_MODEL), lambda i: (i, 0)),
        out_shape=jax.ShapeDtypeStruct((n, D_MODEL), BF16),
        compiler_params=_params("arbitrary"),
        name="prenorm",
    )(x, g, mod, mod)


def _inproj_kernel(h_ref, w_ref, ws_ref, z_ref, zs_ref):
    h = h_ref[...]
    z_ref[...] = _mm(h, w_ref[...]).astype(z_ref.dtype)

    @pl.when(pl.program_id(1) == 0)
    def _():
        zs_ref[...] = _mm(h, ws_ref[...])


def _inproj(h, w_main, w_small, tm, tn):
    n = h.shape[0]
    p = w_main.shape[1]
    ps = w_small.shape[1]
    return pl.pallas_call(
        _inproj_kernel,
        grid=(n // tm, p // tn),
        in_specs=[pl.BlockSpec((tm, D_MODEL), lambda i, j: (i, 0)),
                  pl.BlockSpec((D_MODEL, tn), lambda i, j: (0, j)),
                  pl.BlockSpec((D_MODEL, ps), lambda i, j: (0, 0))],
        out_specs=[pl.BlockSpec((tm, tn), lambda i, j: (i, j)),
                   pl.BlockSpec((tm, ps), lambda i, j: (i, 0))],
        out_shape=[jax.ShapeDtypeStruct((n, p), BF16), jax.ShapeDtypeStruct((n, ps), F32)],
        compiler_params=_params("arbitrary", "arbitrary"),
        name="inproj",
    )(h, w_main, w_small)


def _matmul_kernel(h_ref, w_ref, z_ref):
    z_ref[...] = _mm(h_ref[...], w_ref[...]).astype(z_ref.dtype)


def _upproj(h, w, tm, tn):
    n = h.shape[0]
    p = w.shape[1]
    return pl.pallas_call(
        _matmul_kernel,
        grid=(n // tm, p // tn),
        in_specs=[pl.BlockSpec((tm, D_MODEL), lambda i, j: (i, 0)),
                  pl.BlockSpec((D_MODEL, tn), lambda i, j: (0, j))],
        out_specs=pl.BlockSpec((tm, tn), lambda i, j: (i, j)),
        out_shape=jax.ShapeDtypeStruct((n, p), BF16),
        compiler_params=_params("arbitrary", "arbitrary"),
        name="upproj",
    )(h, w)


def _outproj_kernel(yg_ref, ym_ref, wg_ref, wm_ref, x_ref, gt_ref, g2_ref, sc_ref, sh_ref, x1_ref, h2_ref):
    acc = _mm(yg_ref[...], wg_ref[...]) + _mm(ym_ref[...], wm_ref[...])
    x1 = x_ref[...] + gt_ref[...] * acc
    x1_ref[...] = x1
    h2_ref[...] = (_rms(x1, g2_ref[...]) * (1.0 + sc_ref[...]) + sh_ref[...]).astype(h2_ref.dtype)


def _outproj(yg, ym, w_out, x, g2, mod, layer, seq, ctx_row, tm):
    n = x.shape[0]
    half = w_out.shape[0] // 2
    row = lambda i: (i, 0)
    return pl.pallas_call(
        _outproj_kernel,
        grid=(n // tm,),
        in_specs=[pl.BlockSpec((tm, half), row),
                  pl.BlockSpec((tm, half), row),
                  pl.BlockSpec((half, D_MODEL), lambda i: (0, 0)),
                  pl.BlockSpec((half, D_MODEL), lambda i: (1, 0)),
                  pl.BlockSpec((tm, D_MODEL), row),
                  _mod_spec(layer, 2, seq, ctx_row, tm),
                  pl.BlockSpec((1, D_MODEL), lambda i: (0, 0)),
                  _mod_spec(layer, 4, seq, ctx_row, tm),
                  _mod_spec(layer, 3, seq, ctx_row, tm)],
        out_specs=[pl.BlockSpec((tm, D_MODEL), row), pl.BlockSpec((tm, D_MODEL), row)],
        out_shape=[jax.ShapeDtypeStruct((n, D_MODEL), F32), jax.ShapeDtypeStruct((n, D_MODEL), BF16)],
        compiler_params=_params("arbitrary"),
        name="outproj",
    )(yg, ym, w_out, w_out, x, mod, g2, mod, mod)


def _ffn_kernel(*refs, grid_w, vertical, final, tm, tf):
    if vertical:
        ug_ref, top_ref, bot_ref = refs[:3]
        refs = refs[3:]
    else:
        ug_ref = refs[0]
        refs = refs[1:]
    uv_ref, cw_ref, cb_ref, wd_ref, x1_ref, gt_ref, gn_ref = refs[:7]
    refs = refs[7:]
    if final:
        out_ref, acc_ref, act_ref = refs
    else:
        sc_ref, sh_ref, x2_ref, hn_ref, acc_ref, act_ref = refs
    i = pl.program_id(0)
    f = pl.program_id(1)
    last = pl.num_programs(1) - 1

    @pl.when(f == 0)
    def _():
        acc_ref[...] = jnp.zeros_like(acc_ref)

    n_rows = tm // grid_w
    if vertical:
        blocks_per_batch = (grid_w * grid_w) // tm
        top_ok = (i % blocks_per_batch != 0).astype(F32)
        bot_ok = (i % blocks_per_batch != blocks_per_batch - 1).astype(F32)

    col = lax.broadcasted_iota(jnp.int32, (grid_w, LANE), 0)
    for c in range(tf // LANE):
        lanes = slice(c * LANE, (c + 1) * LANE)
        cw = cw_ref[:, lanes]
        cb = cb_ref[:, lanes]
        for r in range(n_rows):
            rows = slice(r * grid_w, (r + 1) * grid_w)
            mid = ug_ref[rows, lanes].astype(F32)
            if vertical:
                if r > 0:
                    up = ug_ref[(r - 1) * grid_w:r * grid_w, lanes].astype(F32)
                else:
                    up = top_ref[:, lanes].astype(F32) * top_ok
                if r < n_rows - 1:
                    down = ug_ref[(r + 1) * grid_w:(r + 2) * grid_w, lanes].astype(F32)
                else:
                    down = bot_ref[:, lanes].astype(F32) * bot_ok
                taps = [cw[k:k + 1] * up + cw[3 + k:4 + k] * mid + cw[6 + k:7 + k] * down for k in range(3)]
            else:
                taps = [cw[3 + k:4 + k] * mid for k in range(3)]
            left = jnp.where(col != 0, pltpu.roll(taps[0], 1, 0), 0.0)
            right = jnp.where(col != grid_w - 1, pltpu.roll(taps[2], grid_w - 1, 0), 0.0)
            g = taps[1] + left + right + cb
            half_g = 0.5 * g
            silu = half_g * (1.0 + jnp.tanh(half_g))
            act_ref[rows, lanes] = (silu * uv_ref[rows, lanes].astype(F32)).astype(BF16)
    acc_ref[...] += _mm(act_ref[...], wd_ref[...])

    @pl.when(f == last)
    def _():
        x2 = x1_ref[...] + gt_ref[...] * acc_ref[...]
        if final:
            out_ref[...] = _rms(x2, gn_ref[...])
        else:
            x2_ref[...] = x2
            hn_ref[...] = (_rms(x2, gn_ref[...]) * (1.0 + sc_ref[...]) + sh_ref[...]).astype(hn_ref.dtype)


def _ffn(u, conv_w9, conv_b, w_down, x1, mod, layer, gn, seq, ctx_row, grid_w, vertical, final, tm):
    n = x1.shape[0]
    tf = FF_TILE
    nf = D_FF_PAD // tf
    halo = grid_w
    hb = tm // halo
    n_halo_blocks = n // halo
    row = lambda i, f: (i, 0)
    in_specs = [pl.BlockSpec((tm, tf), lambda i, f: (i, f))]
    args = [u]
    if vertical:
        in_specs += [pl.BlockSpec((halo, tf), lambda i, f: (jnp.maximum(i * hb - 1, 0), f)),
                     pl.BlockSpec((halo, tf), lambda i, f: (jnp.minimum(i * hb + hb, n_halo_blocks - 1), f))]
        args += [u, u]
    in_specs += [pl.BlockSpec((tm, tf), lambda i, f: (i, nf + f)),
                 pl.BlockSpec((9, tf), lambda i, f: (0, f)),
                 pl.BlockSpec((1, tf), lambda i, f: (0, f)),
                 pl.BlockSpec((tf, D_MODEL), lambda i, f: (f, 0)),
                 pl.BlockSpec((tm, D_MODEL), row),
                 _mod_spec(layer, 5, seq, ctx_row, tm),
                 pl.BlockSpec((1, D_MODEL), lambda i, f: (0, 0))]
    args += [u, conv_w9, conv_b, w_down, x1, mod, gn]
    if final:
        out_specs = pl.BlockSpec((tm, D_MODEL), row)
        out_shape = jax.ShapeDtypeStruct((n, D_MODEL), F32)
    else:
        in_specs += [_mod_spec(layer + 1, 1, seq, ctx_row, tm),
                     _mod_spec(layer + 1, 0, seq, ctx_row, tm)]
        args += [mod, mod]
        out_specs = [pl.BlockSpec((tm, D_MODEL), row), pl.BlockSpec((tm, D_MODEL), row)]
        out_shape = [jax.ShapeDtypeStruct((n, D_MODEL), F32), jax.ShapeDtypeStruct((n, D_MODEL), BF16)]
    return pl.pallas_call(
        functools.partial(_ffn_kernel, grid_w=grid_w, vertical=vertical, final=final, tm=tm, tf=tf),
        grid=(n // tm, nf),
        in_specs=in_specs,
        out_specs=out_specs,
        out_shape=out_shape,
        scratch_shapes=[pltpu.VMEM((tm, D_MODEL), F32), pltpu.VMEM((tm, tf), BF16)],
        compiler_params=_params("arbitrary", "arbitrary"),
        name="ffn",
    )(*args)


def _decay_kernel(zs_ref, wlr_ref, blr_ref, la_ref):
    glr = zs_ref[:, 0:GLA_LR]
    for d in range(N_DIR):
        la_ref[d] = _log_sigmoid(_mm3(glr, wlr_ref[d]) + blr_ref[d]) * (1.0 / GLA_NORMALIZER)


def _decay(zs, w_lr, b_lr, tm):
    n = zs.shape[0]
    width = w_lr.shape[-1]
    return pl.pallas_call(
        _decay_kernel,
        grid=(n // tm,),
        in_specs=[pl.BlockSpec((tm, zs.shape[1]), lambda i: (i, 0)),
                  pl.BlockSpec(w_lr.shape, lambda i: (0, 0, 0)),
                  pl.BlockSpec(b_lr.shape, lambda i: (0, 0, 0))],
        out_specs=pl.BlockSpec((N_DIR, tm, width), lambda i: (0, i, 0)),
        out_shape=jax.ShapeDtypeStruct((N_DIR, n, width), F32),
        compiler_params=_params("arbitrary"),
        name="decay",
    )(zs, w_lr, b_lr)


def _flip2(a):
    return a[..., ::-1, ::-1]


def _gla_constants():
    c = CHUNK
    m = np.zeros((N_LEVELS + 2, c, c), np.float32)
    later = np.zeros((N_LEVELS, c, 1), np.float32)
    pair = np.zeros((N_LEVELS + 1, c, c), np.float32)
    for lvl in range(N_LEVELS):
        n = c >> lvl
        half = n // 2
        for r in range(c):
            bnd = (r // n) * n + half - 1
            if r % n >= half:
                m[lvl, r, bnd + 1:r + 1] = 1.0
                later[lvl, r] = 1.0
            else:
                m[lvl, r, r + 1:bnd + 1] = 1.0
        blk = np.arange(c) // n
        pos = np.arange(c) % n
        pair[lvl] = (blk[:, None] == blk[None, :]) & (pos[:, None] >= half) & (pos[None, :] < half)
    pair[N_LEVELS] = np.eye(c)
    for r in range(c):
        m[N_LEVELS, r, :r + 1] = 1.0
        m[N_LEVELS + 1, r, r + 1:] = 1.0
    m = np.stack([m, _flip2(m)]).reshape(N_DIR, (N_LEVELS + 2) * c, c)
    later = np.stack([later, later[:, ::-1]])
    later = np.broadcast_to(later, (N_DIR, N_LEVELS, c, DK))
    pair = np.stack([pair, _flip2(pair)])
    return jnp.asarray(m, BF16), jnp.asarray(later, F32), jnp.asarray(pair, F32)


def _mlstm_constants():
    tri = np.tril(np.ones((CHUNK, CHUNK), np.float32))
    return jnp.asarray(np.stack([tri, tri.T]), F32), jnp.asarray(np.eye(CHUNK), F32)


def _drive_scan(n_ctx, n_lat, with_ctx_out, streams):
    def one(stream, d, is_ctx, s, n_total, second_half, want_out):
        chunk, finish, lat_io, ctx_io = stream
        gate_ref, y_ref = ctx_io if is_ctx else lat_io
        n = s if d == 0 else n_total - 1 - s
        start = n * CHUNK if isinstance(n, int) else pl.multiple_of(n * CHUNK, CHUNK)
        rows = pl.ds(start, CHUNK)
        o = yield from chunk(d, is_ctx, n, rows, want_out)
        if not want_out:
            return
        if second_half:
            y_ref[rows, :] = finish(y_ref[rows, :].astype(F32) + o, gate_ref[rows, :])
        else:
            y_ref[rows, :] = o.astype(y_ref.dtype)

    def steps(is_ctx, ss, n_total, second_half, want_out):
        waiting = [[one(stream, d, is_ctx, s, n_total, second_half, want_out)
                    for d in range(N_DIR) for stream in streams] for s in ss]
        live = []
        while live or waiting:
            if waiting:
                live.extend(waiting.pop(0))
            for g in list(live):
                try:
                    next(g)
                except StopIteration:
                    live.remove(g)

    assert n_ctx % 2 == 0 and n_lat % (2 * SCAN_UNROLL) == 0
    for s in range(0, n_ctx, n_ctx // 2):
        steps(True, [s + u for u in range(n_ctx // 2)], n_ctx, s >= n_ctx // 2, with_ctx_out)
    per_half = n_lat // (2 * SCAN_UNROLL)
    for half in range(2):
        def body(k, carry, half=half):
            steps(False, [k * SCAN_UNROLL + u for u in range(SCAN_UNROLL)], n_lat, half == 1, True)
            return carry

        lax.fori_loop(half * per_half, (half + 1) * per_half, body, 0)


def _gla_chunk(d, qc, kc, vc, la, seg_ref, later_ref, pair_ref, st_ref, want_out):
    c = CHUNK
    la_hi, la_lo = _split(la)
    seg = seg_ref[d]
    e_all = jnp.exp(_mm(seg, la_hi) + _mm(seg, la_lo))
    yield
    e_cum = e_all[N_LEVELS * c:(N_LEVELS + 1) * c]
    e_end = e_all[(N_LEVELS + 1) * c:(N_LEVELS + 2) * c]
    last = c - 1 if d == 0 else 0
    decay = e_cum[last:last + 1]
    qf = qc.astype(F32) * (DK ** -0.5)
    kf = kc.astype(F32)
    out = None
    if want_out:
        a = pair_ref[d, N_LEVELS] * _mm(qf.astype(BF16), kc, NT)
        for lvl in range(N_LEVELS):
            e = e_all[lvl * c:(lvl + 1) * c]
            later = later_ref[d, lvl]
            ql = (qf * e * later).astype(BF16)
            kl = (kf * e * (1.0 - later)).astype(BF16)
            a = a + pair_ref[d, lvl] * _mm(ql, kl, NT)
            yield
    st = st_ref[d]
    if want_out:
        out = _mm(a.astype(BF16), vc) + _mm((qf * e_cum).astype(BF16), st.astype(BF16), NT)
    st_ref[d] = st * decay + _mm(vc, (kf * e_end).astype(BF16), TN)
    return out


N_GLA_IN = 14


def _gla_stream(in_refs, y_ref, cy_ref, st_ref):
    (q_ref, k_ref, v_ref, g_ref, la_ref, cq_ref, ck_ref, cv_ref, cg_ref, cla_ref,
     gn_ref, seg_ref, later_ref, pair_ref) = in_refs

    def finish(tot, gate):
        gate = gate.astype(F32)
        return (_rms(tot, gn_ref[...]) * (gate * _sigmoid(gate))).astype(BF16)

    def chunk(d, is_ctx, n, rows, want_out):
        del n
        qr, kr, vr, lr = (cq_ref, ck_ref, cv_ref, cla_ref) if is_ctx else (q_ref, k_ref, v_ref, la_ref)
        return (yield from _gla_chunk(d, qr[rows, :], kr[rows, :], vr[rows, :], lr[d, rows, :],
                                      seg_ref, later_ref, pair_ref, st_ref, want_out))

    st_ref[...] = jnp.zeros_like(st_ref)
    return chunk, finish, (g_ref, y_ref), (cg_ref, cy_ref)


def _gla_specs(t, tc, consts):
    seg, later, pair = consts
    lat = lambda cols, off: pl.BlockSpec((t, cols), lambda b, h, off=off: (b, off + h))
    ctx = lambda cols, off: pl.BlockSpec((tc, cols), lambda b, h, off=off: (b, off + h))
    const = lambda shape: pl.BlockSpec(shape, lambda b, h: (0,) * len(shape))
    return [lat(DK, 0), lat(DK, HEADS), lat(DV, HEADS), lat(DV, 2 * HEADS),
            pl.BlockSpec((N_DIR, t, DK), lambda b, h: (0, b, h)),
            ctx(DK, 0), ctx(DK, HEADS), ctx(DV, HEADS), ctx(DV, 2 * HEADS),
            pl.BlockSpec((N_DIR, tc, DK), lambda b, h: (0, b, h)),
            const((1, DV)), const(seg.shape), const(later.shape), const(pair.shape)]


def _mlstm_chunk(d, qc, kc, vc, gates, tri, tri_t, eye, ct_ref, n_ref, m_ref, want_out):
    li_row = gates[2 * d:2 * d + 1]
    lf_row = _log_sigmoid(gates[2 * d + 1:2 * d + 2])
    li_col = jnp.sum(eye * li_row, axis=1, keepdims=True)
    lf_col = jnp.sum(eye * lf_row, axis=1, keepdims=True)
    fcum_col = jnp.sum(tri * lf_row, axis=1, keepdims=True)
    fcum_row = jnp.sum(tri_t * lf_col, axis=0, keepdims=True)
    f_end = jnp.sum(lf_row, axis=1, keepdims=True)
    yield
    m = m_ref[d][:, 0:1]
    m_new = jnp.maximum(f_end + m, jnp.max(f_end - fcum_row + li_row, axis=1, keepdims=True))
    carry = jnp.exp(f_end + m - m_new)
    w_col = jnp.exp(f_end - fcum_col + li_col - m_new)
    m_ref[d] = jnp.broadcast_to(m_new, m_ref.shape[1:])
    kf = kc.astype(F32) * (DK ** -0.5)
    kw = kf * w_col
    yield
    out = None
    if want_out:
        qf = qc.astype(F32)
        d_log = fcum_col - fcum_row + li_row
        causal = tri > 0.5
        inter_log = fcum_col + m
        m_q = jnp.maximum(inter_log, jnp.max(jnp.where(causal, d_log, -jnp.inf), axis=1, keepdims=True))
        p = jnp.where(causal, jnp.exp(d_log - m_q), 0.0)
        s = _mm(qc, kf.astype(BF16), NT) * p
        inter = jnp.exp(inter_log - m_q)
        yield
    ct = ct_ref[d]
    nvec = n_ref[d]
    if want_out:
        num = _mm(s.astype(BF16), vc) + inter * _mm(qc, ct.astype(BF16), NT)
        den = jnp.sum(s, axis=1, keepdims=True) + inter * jnp.sum(qf * nvec, axis=1, keepdims=True)
        out = num / jnp.maximum(jnp.abs(den), jnp.exp(-m_q))
    ct_ref[d] = carry * ct + _mm(vc, kw.astype(BF16), TN)
    n_ref[d] = carry * nvec + jnp.sum(kw, axis=0, keepdims=True)
    return out


N_MLSTM_IN = 14


def _mlstm_stream(in_refs, y_ref, cy_ref, ct_ref, n_ref, m_ref):
    (q_ref, k_ref, v_ref, g_ref, gr_ref, cq_ref, ck_ref, cv_ref, cg_ref, cgr_ref,
     bias_ref, gn_ref, tri_ref, eye_ref) = in_refs

    def finish(tot, gate):
        return (_sigmoid(gate.astype(F32)) * _rms(tot, gn_ref[...])).astype(BF16)

    def chunk(d, is_ctx, n, rows, want_out):
        qr, kr, vr, grr = (cq_ref, ck_ref, cv_ref, cgr_ref) if is_ctx else (q_ref, k_ref, v_ref, gr_ref)
        return (yield from _mlstm_chunk(d, qr[rows, :], kr[rows, :], vr[rows, :], grr[n] + bias_ref[...],
                                        tri_ref[d], tri_ref[1 - d], eye_ref[...], ct_ref, n_ref, m_ref, want_out))

    ct_ref[...] = jnp.zeros_like(ct_ref)
    n_ref[...] = jnp.zeros_like(n_ref)
    m_ref[...] = jnp.zeros_like(m_ref)
    return chunk, finish, (g_ref, y_ref), (cg_ref, cy_ref)


def _gate_rows(zs):
    n = zs.shape[0]
    g = zs[:, GLA_LR:].reshape(n // CHUNK, CHUNK, N_DIR * 2, HEADS)
    return jnp.transpose(g, (3, 0, 2, 1))


def _mlstm_specs(t, tc, consts):
    tri, eye = consts
    base = (2 * HEADS * DK + 2 * HEADS * DV)
    bq, bv = base // DK, (base + 2 * HEADS * DK) // DV
    lat = lambda cols, off: pl.BlockSpec((t, cols), lambda b, h, off=off: (b, off + h))
    ctx = lambda cols, off: pl.BlockSpec((tc, cols), lambda b, h, off=off: (b, off + h))
    const = lambda shape: pl.BlockSpec(shape, lambda b, h: (0,) * len(shape))
    return [lat(DK, bq), lat(DK, bq + HEADS), lat(DV, bv), lat(DV, bv + HEADS),
            pl.BlockSpec((None, t // CHUNK, 2 * N_DIR, CHUNK), lambda b, h: (h, b, 0, 0)),
            ctx(DK, bq), ctx(DK, bq + HEADS), ctx(DV, bv), ctx(DV, bv + HEADS),
            pl.BlockSpec((None, tc // CHUNK, 2 * N_DIR, CHUNK), lambda b, h: (h, b, 0, 0)),
            pl.BlockSpec((None, 2 * N_DIR, 1), lambda b, h: (h, 0, 0)),
            const((1, DV)), const(tri.shape), const(eye.shape)]


def _scan_kernel(*refs, with_ctx_out):
    gla_in = refs[:N_GLA_IN]
    ml_in = refs[N_GLA_IN:N_GLA_IN + N_MLSTM_IN]
    rest = refs[N_GLA_IN + N_MLSTM_IN:]
    if with_ctx_out:
        yg_ref, ym_ref, cyg_ref, cym_ref, st_ref, ct_ref, n_ref, m_ref = rest
    else:
        yg_ref, ym_ref, st_ref, ct_ref, n_ref, m_ref = rest
        cyg_ref = cym_ref = None
    streams = [_gla_stream(gla_in, yg_ref, cyg_ref, st_ref),
               _mlstm_stream(ml_in, ym_ref, cym_ref, ct_ref, n_ref, m_ref)]
    _drive_scan(gla_in[5].shape[0] // CHUNK, gla_in[0].shape[0] // CHUNK, with_ctx_out, streams)


def _scans(z, zs, la, cz, czs, cla, gla_g_norm, b_gate, mlstm_g_norm, gla_consts, mlstm_consts, batch,
           with_ctx_out):
    t = z.shape[0] // batch
    tc = cz.shape[0] // batch
    in_specs = _gla_specs(t, tc, gla_consts) + _mlstm_specs(t, tc, mlstm_consts)
    head_out = lambda rows: pl.BlockSpec((rows, DV), lambda b, h: (b, h))
    out_specs = [head_out(t), head_out(t)]
    out_shape = [jax.ShapeDtypeStruct((batch * t, HEADS * DV), BF16)] * 2
    scratch = [pltpu.VMEM((N_DIR, DV, DK), F32), pltpu.VMEM((N_DIR, DV, DK), F32),
               pltpu.VMEM((N_DIR, 1, DK), F32), pltpu.VMEM((N_DIR, 1, DK), F32)]
    if with_ctx_out:
        out_specs += [head_out(tc), head_out(tc)]
        out_shape += [jax.ShapeDtypeStruct((batch * tc, HEADS * DV), BF16)] * 2
    bias = jnp.transpose(b_gate.reshape(2 * N_DIR, HEADS))[:, :, None]
    return pl.pallas_call(
        functools.partial(_scan_kernel, with_ctx_out=with_ctx_out),
        grid=(batch, HEADS),
        in_specs=in_specs,
        out_specs=out_specs,
        out_shape=out_shape,
        scratch_shapes=scratch,
        compiler_params=_params("arbitrary", "arbitrary"),
        name="scans",
    )(z, z, z, z, la, cz, cz, cz, cz, cla, gla_g_norm, *gla_consts,
      z, z, z, z, _gate_rows(zs), cz, cz, cz, cz, _gate_rows(czs), bias, mlstm_g_norm, *mlstm_consts)


def _prep_layer_weights(w_in, w_out, w_up, conv_w, conv_b, w_down):
    hk, hv = HEADS * DK, HEADS * DV
    glr0 = 2 * hk + 2 * hv
    ml0 = glr0 + GLA_LR
    gate0 = ml0 + 2 * hk + 2 * hv
    w_main = jnp.concatenate([w_in[:, :glr0], w_in[:, ml0:gate0]], axis=1).astype(BF16)
    w_small = jnp.concatenate([w_in[:, glr0:ml0], w_in[:, gate0:]], axis=1).astype(BF16)
    pad = D_FF_PAD - D_FF
    zc = jnp.zeros((D_MODEL, pad), w_up.dtype)
    w_up_p = jnp.concatenate([w_up[:, :D_FF], zc, w_up[:, D_FF:], zc], axis=1).astype(BF16)
    conv_w9 = jnp.pad(conv_w.reshape(9, D_FF), ((0, 0), (0, pad)))
    conv_bp = jnp.pad(conv_b.reshape(1, D_FF), ((0, 0), (0, pad)))
    w_down_p = jnp.pad(w_down, ((0, pad), (0, 0))).astype(BF16)
    return w_main, w_small, w_out.astype(BF16), w_up_p, conv_w9, conv_bp, w_down_p


def kernel(x, c, ctx, c_ctx, w_mod, b_mod, g_norm1, g_norm2, w_in, gla_w_lr, gla_b_lr, mlstm_b_gate,
           gla_g_norm, mlstm_g_norm, w_out, w_up, conv_w, conv_b, w_down, g_final):
    batch, seq, _ = x.shape
    ctx_len = ctx.shape[1]
    assert batch < MOD_ROWS and seq == GRID_W * GRID_W and ctx_len % CHUNK == 0
    tm_lat = 512
    tm_ctx = min(512, batch * ctx_len)
    cc = jnp.zeros((MOD_ROWS, D_MODEL), F32).at[:batch].set(c).at[batch].set(c_ctx)
    mod = _modulation(cc, w_mod, b_mod)
    gla_consts = _gla_constants()
    mlstm_consts = _mlstm_constants()

    xl = x.reshape(batch * seq, D_MODEL)
    xc = ctx.reshape(batch * ctx_len, D_MODEL)
    row = lambda v: v.reshape(1, -1)
    hl = _prenorm(xl, row(g_norm1[0]), mod, 0, seq, None, tm_lat)
    hc = _prenorm(xc, row(g_norm1[0]), mod, 0, tm_ctx, batch, tm_ctx)
    out = None
    for l in range(DEPTH):
        last = l == DEPTH - 1
        w_main, w_small, w_out_b, w_up_p, conv_w9, conv_bp, w_down_p = _prep_layer_weights(
            w_in[l], w_out[l], w_up[l], conv_w[l], conv_b[l], w_down[l])
        z, zs = _inproj(hl, w_main, w_small, 1024, 1024)
        cz, czs = _inproj(hc, w_main, w_small, tm_ctx, 1024)
        b_lr = gla_b_lr[l].reshape(N_DIR, 1, HEADS * DK)
        la = _decay(zs, gla_w_lr[l], b_lr, 1024)
        cla = _decay(czs, gla_w_lr[l], b_lr, tm_ctx)
        ys = _scans(z, zs, la, cz, czs, cla, row(gla_g_norm[l]), mlstm_b_gate[l],
                    row(mlstm_g_norm[l]), gla_consts, mlstm_consts, batch, not last)
        x1, h2 = _outproj(ys[0], ys[1], w_out_b, xl, row(g_norm2[l]), mod, l, seq, None, 256)
        u = _upproj(h2, w_up_p, 1024, 1024)
        if last:
            out = _ffn(u, conv_w9, conv_bp, w_down_p, x1, mod, l, row(g_final), seq, None, GRID_W, True, True, tm_lat)
        else:
            xl, hl = _ffn(u, conv_w9, conv_bp, w_down_p, x1, mod, l, row(g_norm1[l + 1]), seq, None,
                          GRID_W, True, False, tm_lat)
            c1, hc2 = _outproj(ys[2], ys[3], w_out_b, xc, row(g_norm2[l]), mod, l, 256, batch, 256)
            cu = _upproj(hc2, w_up_p, tm_ctx, 1024)
            xc, hc = _ffn(cu, conv_w9, conv_bp, w_down_p, c1, mod, l, row(g_norm1[l + 1]), tm_ctx, batch,
                          ctx_len, False, False, tm_ctx)
    return out.reshape(batch, seq, D_MODEL)
```

```python
import functools

import numpy as np
import jax
import jax.numpy as jnp
from jax import lax
from jax.experimental import pallas as pl
from jax.experimental.pallas import tpu as pltpu

D_MODEL = 2048
DEPTH = 2
GRID_W = 64
N_DIR = 2
HEADS = 4
DK = 128
DV = 256
GLA_LR = 16
GLA_NORMALIZER = 16.0
CHUNK = 64
D_FF = 5504
D_FF_PAD = 5632
FF_TILE = 512
UPCONV_TILE = 256
EPS = 1e-6
LANE = 128
SCAN_UNROLL = 4
N_LEVELS = 6
N_MOD = 6
MOD_ROWS = 8

VMEM_LIMIT = 56 * 1024 * 1024

F32 = jnp.float32
BF16 = jnp.bfloat16
NN = (((1,), (0,)), ((), ()))
NT = (((1,), (1,)), ((), ()))
TN = (((0,), (0,)), ((), ()))


def _mm(a, b, dims=NN):
    return lax.dot_general(a, b, dims, preferred_element_type=F32)


def _split(x):
    hi = x.astype(BF16)
    return hi, (x - hi.astype(F32)).astype(BF16)


def _mm3(a, b):
    ah, al = _split(a)
    bh, bl = _split(b)
    return _mm(ah, bh) + (_mm(ah, bl) + _mm(al, bh))


def _sigmoid(x):
    return 1.0 / (1.0 + jnp.exp(-x))


def _log_sigmoid(x):
    return jnp.minimum(x, 0.0) - jnp.log(1.0 + jnp.exp(-jnp.abs(x)))


def _rms(x, g):
    return x * lax.rsqrt(jnp.mean(x * x, axis=-1, keepdims=True) + EPS) * g


def _params(*sem):
    return pltpu.CompilerParams(dimension_semantics=sem, vmem_limit_bytes=VMEM_LIMIT)


def _mod_kernel(cc_ref, w_ref, b_ref, o_ref):
    a = cc_ref[...]
    a = a * _sigmoid(a)
    o_ref[...] = _mm3(a, w_ref[...]) + b_ref[...]


def _modulation(cc, w_mod, b_mod):
    tn = 1024
    out = pl.pallas_call(
        _mod_kernel,
        grid=(DEPTH, N_MOD * D_MODEL // tn),
        in_specs=[pl.BlockSpec((MOD_ROWS, D_MODEL), lambda l, j: (0, 0)),
                  pl.BlockSpec((None, D_MODEL, tn), lambda l, j: (l, 0, j)),
                  pl.BlockSpec((None, 1, tn), lambda l, j: (l, 0, j))],
        out_specs=pl.BlockSpec((None, MOD_ROWS, tn), lambda l, j: (l, 0, j)),
        out_shape=jax.ShapeDtypeStruct((DEPTH, MOD_ROWS, N_MOD * D_MODEL), F32),
        compiler_params=_params("arbitrary", "arbitrary"),
        name="modulation",
    )(cc, w_mod, b_mod.reshape(DEPTH, 1, N_MOD * D_MODEL))
    return out.reshape(DEPTH * MOD_ROWS * N_MOD, 1, D_MODEL)


def _mod_spec(layer, piece, seq, ctx_row, tm):
    base = layer * MOD_ROWS * N_MOD + piece
    assert seq % tm == 0
    blocks_per_batch = seq // tm

    def idx(i, *_):
        r = ctx_row if ctx_row is not None else i // blocks_per_batch
        return (base + r * N_MOD, 0, 0)

    return pl.BlockSpec((None, 1, D_MODEL), idx)


def _prenorm_kernel(x_ref, g_ref, sc_ref, sh_ref, o_ref):
    o_ref[...] = (_rms(x_ref[...], g_ref[...]) * (1.0 + sc_ref[...]) + sh_ref[...]).astype(o_ref.dtype)


def _prenorm(x, g, mod, layer, seq, ctx_row, tm):
    n = x.shape[0]
    return pl.pallas_call(
        _prenorm_kernel,
        grid=(n // tm,),
        in_specs=[pl.BlockSpec((tm, D_MODEL), lambda i: (i, 0)),
                  pl.BlockSpec((1, D_MODEL), lambda i: (0, 0)),
                  _mod_spec(layer, 1, seq, ctx_row, tm),
                  _mod_spec(layer, 0, seq, ctx_row, tm)],
        out_specs=pl.BlockSpec((tm, D_MODEL), lambda i: (i, 0)),
        out_shape=jax.ShapeDtypeStruct((n, D_MODEL), BF16),
        compiler_params=_params("arbitrary"),
        name="prenorm",
    )(x, g, mod, mod)


def _inproj_kernel(h_ref, w_ref, ws_ref, z_ref, zs_ref):
    h = h_ref[...]
    z_ref[...] = _mm(h, w_ref[...]).astype(z_ref.dtype)

    @pl.when(pl.program_id(1) == 0)
    def _():
        zs_ref[...] = _mm(h, ws_ref[...])


def _inproj(h, w_main, w_small, tm, tn):
    n = h.shape[0]
    p = w_main.shape[1]
    ps = w_small.shape[1]
    return pl.pallas_call(
        _inproj_kernel,
        grid=(n // tm, p // tn),
        in_specs=[pl.BlockSpec((tm, D_MODEL), lambda i, j: (i, 0)),
                  pl.BlockSpec((D_MODEL, tn), lambda i, j: (0, j)),
                  pl.BlockSpec((D_MODEL, ps), lambda i, j: (0, 0))],
        out_specs=[pl.BlockSpec((tm, tn), lambda i, j: (i, j)),
                   pl.BlockSpec((tm, ps), lambda i, j: (i, 0))],
        out_shape=[jax.ShapeDtypeStruct((n, p), BF16), jax.ShapeDtypeStruct((n, ps), F32)],
        compiler_params=_params("arbitrary", "arbitrary"),
        name="inproj",
    )(h, w_main, w_small)


def _outproj_kernel(yg_ref, ym_ref, wg_ref, wm_ref, x_ref, gt_ref, g2_ref, sc_ref, sh_ref, x1_ref, h2_ref):
    acc = _mm(yg_ref[...], wg_ref[...]) + _mm(ym_ref[...], wm_ref[...])
    x1 = x_ref[...] + gt_ref[...] * acc
    x1_ref[...] = x1
    h2_ref[...] = (_rms(x1, g2_ref[...]) * (1.0 + sc_ref[...]) + sh_ref[...]).astype(h2_ref.dtype)


def _outproj(yg, ym, w_out, x, g2, mod, layer, seq, ctx_row, tm):
    n = x.shape[0]
    half = w_out.shape[0] // 2
    row = lambda i: (i, 0)
    return pl.pallas_call(
        _outproj_kernel,
        grid=(n // tm,),
        in_specs=[pl.BlockSpec((tm, half), row),
                  pl.BlockSpec((tm, half), row),
                  pl.BlockSpec((half, D_MODEL), lambda i: (0, 0)),
                  pl.BlockSpec((half, D_MODEL), lambda i: (1, 0)),
                  pl.BlockSpec((tm, D_MODEL), row),
                  _mod_spec(layer, 2, seq, ctx_row, tm),
                  pl.BlockSpec((1, D_MODEL), lambda i: (0, 0)),
                  _mod_spec(layer, 4, seq, ctx_row, tm),
                  _mod_spec(layer, 3, seq, ctx_row, tm)],
        out_specs=[pl.BlockSpec((tm, D_MODEL), row), pl.BlockSpec((tm, D_MODEL), row)],
        out_shape=[jax.ShapeDtypeStruct((n, D_MODEL), F32), jax.ShapeDtypeStruct((n, D_MODEL), BF16)],
        compiler_params=_params("arbitrary"),
        name="outproj",
    )(yg, ym, w_out, w_out, x, mod, g2, mod, mod)


def _upconv_kernel(h_ref, wg_ref, wv_ref, cw_ref, cb_ref, act_ref, ug_ref, uv_ref, *, grid_w, chunk_rows):
    n_tok, tf = ug_ref.shape
    n_rows = n_tok // grid_w
    rows_per_chunk = chunk_rows // grid_w
    n_chunks = n_tok // chunk_rows
    col = lax.broadcasted_iota(jnp.int32, (grid_w, LANE), 0)

    def project(k):
        rows = slice(k * chunk_rows, (k + 1) * chunk_rows)
        h = h_ref[rows, :]
        ug_ref[rows, :] = _mm(h, wg_ref[...]).astype(ug_ref.dtype)
        uv_ref[rows, :] = _mm(h, wv_ref[...]).astype(uv_ref.dtype)

    def conv(k):
        for c in range(tf // LANE):
            lanes = slice(c * LANE, (c + 1) * LANE)
            cw = cw_ref[:, lanes]
            cb = cb_ref[:, lanes]
            for r in range(k * rows_per_chunk, (k + 1) * rows_per_chunk):
                rows = slice(r * grid_w, (r + 1) * grid_w)
                mid = ug_ref[rows, lanes].astype(F32)
                taps = [cw[3 + j:4 + j] * mid for j in range(3)]
                if r > 0:
                    up = ug_ref[(r - 1) * grid_w:r * grid_w, lanes].astype(F32)
                    taps = [t + cw[j:j + 1] * up for j, t in enumerate(taps)]
                if r < n_rows - 1:
                    down = ug_ref[(r + 1) * grid_w:(r + 2) * grid_w, lanes].astype(F32)
                    taps = [t + cw[6 + j:7 + j] * down for j, t in enumerate(taps)]
                left = jnp.where(col != 0, pltpu.roll(taps[0], 1, 0), 0.0)
                right = jnp.where(col != grid_w - 1, pltpu.roll(taps[2], grid_w - 1, 0), 0.0)
                g = taps[1] + left + right + cb
                half_g = 0.5 * g
                silu = half_g * (1.0 + jnp.tanh(half_g))
                act_ref[rows, lanes] = (silu * uv_ref[rows, lanes].astype(F32)).astype(act_ref.dtype)

    project(0)
    for k in range(1, n_chunks):
        project(k)
        conv(k - 1)
    conv(n_chunks - 1)


def _upconv(h, w_up, conv_w9, conv_b, img_tokens, grid_w, chunk_rows):
    n = h.shape[0]
    tf = UPCONV_TILE
    nf = D_FF_PAD // tf
    return pl.pallas_call(
        functools.partial(_upconv_kernel, grid_w=grid_w, chunk_rows=chunk_rows),
        grid=(n // img_tokens, nf),
        in_specs=[pl.BlockSpec((img_tokens, D_MODEL), lambda b, f: (b, 0)),
                  pl.BlockSpec((D_MODEL, tf), lambda b, f: (0, f)),
                  pl.BlockSpec((D_MODEL, tf), lambda b, f: (0, nf + f)),
                  pl.BlockSpec((9, tf), lambda b, f: (0, f)),
                  pl.BlockSpec((1, tf), lambda b, f: (0, f))],
        out_specs=pl.BlockSpec((img_tokens, tf), lambda b, f: (b, f)),
        out_shape=jax.ShapeDtypeStruct((n, D_FF_PAD), BF16),
        scratch_shapes=[pltpu.VMEM((img_tokens, tf), BF16), pltpu.VMEM((img_tokens, tf), BF16)],
        compiler_params=_params("arbitrary", "arbitrary"),
        name="upconv",
    )(h, w_up, w_up, conv_w9, conv_b)


def _down_kernel(*refs, final):
    act_ref, wd_ref, x1_ref, gt_ref, gn_ref = refs[:5]
    if final:
        out_ref, acc_ref = refs[5:]
    else:
        sc_ref, sh_ref, x2_ref, hn_ref, acc_ref = refs[5:]
    f = pl.program_id(1)

    @pl.when(f == 0)
    def _():
        acc_ref[...] = jnp.zeros_like(acc_ref)

    acc_ref[...] += _mm(act_ref[...], wd_ref[...])

    @pl.when(f == pl.num_programs(1) - 1)
    def _():
        x2 = x1_ref[...] + gt_ref[...] * acc_ref[...]
        if final:
            out_ref[...] = _rms(x2, gn_ref[...])
        else:
            x2_ref[...] = x2
            hn_ref[...] = (_rms(x2, gn_ref[...]) * (1.0 + sc_ref[...]) + sh_ref[...]).astype(hn_ref.dtype)


def _down(act, w_down, x1, mod, layer, gn, seq, ctx_row, final, tm):
    n = x1.shape[0]
    tk = FF_TILE
    row = lambda i, f: (i, 0)
    in_specs = [pl.BlockSpec((tm, tk), lambda i, f: (i, f)),
                pl.BlockSpec((tk, D_MODEL), lambda i, f: (f, 0)),
                pl.BlockSpec((tm, D_MODEL), row),
                _mod_spec(layer, 5, seq, ctx_row, tm),
                pl.BlockSpec((1, D_MODEL), lambda i, f: (0, 0))]
    args = [act, w_down, x1, mod, gn]
    if final:
        out_specs = pl.BlockSpec((tm, D_MODEL), row)
        out_shape = jax.ShapeDtypeStruct((n, D_MODEL), F32)
    else:
        in_specs += [_mod_spec(layer + 1, 1, seq, ctx_row, tm),
                     _mod_spec(layer + 1, 0, seq, ctx_row, tm)]
        args += [mod, mod]
        out_specs = [pl.BlockSpec((tm, D_MODEL), row), pl.BlockSpec((tm, D_MODEL), row)]
        out_shape = [jax.ShapeDtypeStruct((n, D_MODEL), F32), jax.ShapeDtypeStruct((n, D_MODEL), BF16)]
    return pl.pallas_call(
        functools.partial(_down_kernel, final=final),
        grid=(n // tm, D_FF_PAD // tk),
        in_specs=in_specs,
        out_specs=out_specs,
        out_shape=out_shape,
        scratch_shapes=[pltpu.VMEM((tm, D_MODEL), F32)],
        compiler_params=_params("arbitrary", "arbitrary"),
        name="down",
    )(*args)


def _decay_kernel(zs_ref, wlr_ref, blr_ref, la_ref):
    glr = zs_ref[:, 0:GLA_LR]
    for d in range(N_DIR):
        la_ref[d] = _log_sigmoid(_mm3(glr, wlr_ref[d]) + blr_ref[d]) * (1.0 / GLA_NORMALIZER)


def _decay(zs, w_lr, b_lr, tm):
    n = zs.shape[0]
    width = w_lr.shape[-1]
    return pl.pallas_call(
        _decay_kernel,
        grid=(n // tm,),
        in_specs=[pl.BlockSpec((tm, zs.shape[1]), lambda i: (i, 0)),
                  pl.BlockSpec(w_lr.shape, lambda i: (0, 0, 0)),
                  pl.BlockSpec(b_lr.shape, lambda i: (0, 0, 0))],
        out_specs=pl.BlockSpec((N_DIR, tm, width), lambda i: (0, i, 0)),
        out_shape=jax.ShapeDtypeStruct((N_DIR, n, width), F32),
        compiler_params=_params("arbitrary"),
        name="decay",
    )(zs, w_lr, b_lr)


def _flip2(a):
    return a[..., ::-1, ::-1]


def _gla_constants():
    c = CHUNK
    m = np.zeros((N_LEVELS + 2, c, c), np.float32)
    later = np.zeros((N_LEVELS, c, 1), np.float32)
    pair = np.zeros((N_LEVELS + 1, c, c), np.float32)
    for lvl in range(N_LEVELS):
        n = c >> lvl
        half = n // 2
        for r in range(c):
            bnd = (r // n) * n + half - 1
            if r % n >= half:
                m[lvl, r, bnd + 1:r + 1] = 1.0
                later[lvl, r] = 1.0
            else:
                m[lvl, r, r + 1:bnd + 1] = 1.0
        blk = np.arange(c) // n
        pos = np.arange(c) % n
        pair[lvl] = (blk[:, None] == blk[None, :]) & (pos[:, None] >= half) & (pos[None, :] < half)
    pair[N_LEVELS] = np.eye(c)
    for r in range(c):
        m[N_LEVELS, r, :r + 1] = 1.0
        m[N_LEVELS + 1, r, r + 1:] = 1.0
    m = np.stack([m, _flip2(m)]).reshape(N_DIR, (N_LEVELS + 2) * c, c)
    later = np.stack([later, later[:, ::-1]])
    later = np.broadcast_to(later, (N_DIR, N_LEVELS, c, DK))
    pair = np.stack([pair, _flip2(pair)])
    return jnp.asarray(m, BF16), jnp.asarray(later, F32), jnp.asarray(pair, F32)


def _mlstm_constants():
    tri = np.tril(np.ones((CHUNK, CHUNK), np.float32))
    return jnp.asarray(np.stack([tri, tri.T]), F32), jnp.asarray(np.eye(CHUNK), F32)


def _drive_scan(n_ctx, n_lat, with_ctx_out, streams):
    def one(stream, d, is_ctx, s, n_total, second_half, want_out):
        chunk, finish, lat_io, ctx_io = stream
        gate_ref, y_ref = ctx_io if is_ctx else lat_io
        n = s if d == 0 else n_total - 1 - s
        start = n * CHUNK if isinstance(n, int) else pl.multiple_of(n * CHUNK, CHUNK)
        rows = pl.ds(start, CHUNK)
        o = yield from chunk(d, is_ctx, n, rows, want_out)
        if not want_out:
            return
        if second_half:
            y_ref[rows, :] = finish(y_ref[rows, :].astype(F32) + o, gate_ref[rows, :])
        else:
            y_ref[rows, :] = o.astype(y_ref.dtype)

    def steps(is_ctx, ss, n_total, second_half, want_out):
        waiting = [[one(stream, d, is_ctx, s, n_total, second_half, want_out)
                    for d in range(N_DIR) for stream in streams] for s in ss]
        live = []
        while live or waiting:
            if waiting:
                live.extend(waiting.pop(0))
            for g in list(live):
                try:
                    next(g)
                except StopIteration:
                    live.remove(g)

    assert n_ctx % 2 == 0 and n_lat % (2 * SCAN_UNROLL) == 0
    for s in range(0, n_ctx, n_ctx // 2):
        steps(True, [s + u for u in range(n_ctx // 2)], n_ctx, s >= n_ctx // 2, with_ctx_out)
    per_half = n_lat // (2 * SCAN_UNROLL)
    for half in range(2):
        def body(k, carry, half=half):
            steps(False, [k * SCAN_UNROLL + u for u in range(SCAN_UNROLL)], n_lat, half == 1, True)
            return carry

        lax.fori_loop(half * per_half, (half + 1) * per_half, body, 0)


def _gla_chunk(d, qc, kc, vc, la, seg_ref, later_ref, pair_ref, st_ref, want_out):
    c = CHUNK
    la_hi, la_lo = _split(la)
    seg = seg_ref[d]
    e_all = jnp.exp(_mm(seg, la_hi) + _mm(seg, la_lo))
    yield
    e_cum = e_all[N_LEVELS * c:(N_LEVELS + 1) * c]
    e_end = e_all[(N_LEVELS + 1) * c:(N_LEVELS + 2) * c]
    last = c - 1 if d == 0 else 0
    decay = e_cum[last:last + 1]
    qf = qc.astype(F32) * (DK ** -0.5)
    kf = kc.astype(F32)
    out = None
    if want_out:
        a = pair_ref[d, N_LEVELS] * _mm(qf.astype(BF16), kc, NT)
        for lvl in range(N_LEVELS):
            e = e_all[lvl * c:(lvl + 1) * c]
            later = later_ref[d, lvl]
            ql = (qf * e * later).astype(BF16)
            kl = (kf * e * (1.0 - later)).astype(BF16)
            a = a + pair_ref[d, lvl] * _mm(ql, kl, NT)
            yield
    st = st_ref[d]
    if want_out:
        out = _mm(a.astype(BF16), vc) + _mm((qf * e_cum).astype(BF16), st.astype(BF16), NT)
    st_ref[d] = st * decay + _mm(vc, (kf * e_end).astype(BF16), TN)
    return out


N_GLA_IN = 14


def _gla_stream(in_refs, y_ref, cy_ref, st_ref):
    (q_ref, k_ref, v_ref, g_ref, la_ref, cq_ref, ck_ref, cv_ref, cg_ref, cla_ref,
     gn_ref, seg_ref, later_ref, pair_ref) = in_refs

    def finish(tot, gate):
        gate = gate.astype(F32)
        return (_rms(tot, gn_ref[...]) * (gate * _sigmoid(gate))).astype(BF16)

    def chunk(d, is_ctx, n, rows, want_out):
        del n
        qr, kr, vr, lr = (cq_ref, ck_ref, cv_ref, cla_ref) if is_ctx else (q_ref, k_ref, v_ref, la_ref)
        return (yield from _gla_chunk(d, qr[rows, :], kr[rows, :], vr[rows, :], lr[d, rows, :],
                                      seg_ref, later_ref, pair_ref, st_ref, want_out))

    st_ref[...] = jnp.zeros_like(st_ref)
    return chunk, finish, (g_ref, y_ref), (cg_ref, cy_ref)


def _gla_specs(t, tc, consts):
    seg, later, pair = consts
    lat = lambda cols, off: pl.BlockSpec((t, cols), lambda b, h, off=off: (b, off + h))
    ctx = lambda cols, off: pl.BlockSpec((tc, cols), lambda b, h, off=off: (b, off + h))
    const = lambda shape: pl.BlockSpec(shape, lambda b, h: (0,) * len(shape))
    return [lat(DK, 0), lat(DK, HEADS), lat(DV, HEADS), lat(DV, 2 * HEADS),
            pl.BlockSpec((N_DIR, t, DK), lambda b, h: (0, b, h)),
            ctx(DK, 0), ctx(DK, HEADS), ctx(DV, HEADS), ctx(DV, 2 * HEADS),
            pl.BlockSpec((N_DIR, tc, DK), lambda b, h: (0, b, h)),
            const((1, DV)), const(seg.shape), const(later.shape), const(pair.shape)]


def _mlstm_chunk(d, qc, kc, vc, gates, tri, tri_t, eye, ct_ref, n_ref, m_ref, want_out):
    li_row = gates[2 * d:2 * d + 1]
    lf_row = _log_sigmoid(gates[2 * d + 1:2 * d + 2])
    li_col = jnp.sum(eye * li_row, axis=1, keepdims=True)
    lf_col = jnp.sum(eye * lf_row, axis=1, keepdims=True)
    fcum_col = jnp.sum(tri * lf_row, axis=1, keepdims=True)
    fcum_row = jnp.sum(tri_t * lf_col, axis=0, keepdims=True)
    f_end = jnp.sum(lf_row, axis=1, keepdims=True)
    yield
    m = m_ref[d][:, 0:1]
    m_new = jnp.maximum(f_end + m, jnp.max(f_end - fcum_row + li_row, axis=1, keepdims=True))
    carry = jnp.exp(f_end + m - m_new)
    w_col = jnp.exp(f_end - fcum_col + li_col - m_new)
    m_ref[d] = jnp.broadcast_to(m_new, m_ref.shape[1:])
    kf = kc.astype(F32) * (DK ** -0.5)
    kw = kf * w_col
    yield
    out = None
    if want_out:
        qf = qc.astype(F32)
        d_log = fcum_col - fcum_row + li_row
        causal = tri > 0.5
        inter_log = fcum_col + m
        m_q = jnp.maximum(inter_log, jnp.max(jnp.where(causal, d_log, -jnp.inf), axis=1, keepdims=True))
        p = jnp.where(causal, jnp.exp(d_log - m_q), 0.0)
        s = _mm(qc, kf.astype(BF16), NT) * p
        inter = jnp.exp(inter_log - m_q)
        yield
    ct = ct_ref[d]
    nvec = n_ref[d]
    if want_out:
        num = _mm(s.astype(BF16), vc) + inter * _mm(qc, ct.astype(BF16), NT)
        den = jnp.sum(s, axis=1, keepdims=True) + inter * jnp.sum(qf * nvec, axis=1, keepdims=True)
        out = num / jnp.maximum(jnp.abs(den), jnp.exp(-m_q))
    ct_ref[d] = carry * ct + _mm(vc, kw.astype(BF16), TN)
    n_ref[d] = carry * nvec + jnp.sum(kw, axis=0, keepdims=True)
    return out


N_MLSTM_IN = 14


def _mlstm_stream(in_refs, y_ref, cy_ref, ct_ref, n_ref, m_ref):
    (q_ref, k_ref, v_ref, g_ref, gr_ref, cq_ref, ck_ref, cv_ref, cg_ref, cgr_ref,
     bias_ref, gn_ref, tri_ref, eye_ref) = in_refs

    def finish(tot, gate):
        return (_sigmoid(gate.astype(F32)) * _rms(tot, gn_ref[...])).astype(BF16)

    def chunk(d, is_ctx, n, rows, want_out):
        qr, kr, vr, grr = (cq_ref, ck_ref, cv_ref, cgr_ref) if is_ctx else (q_ref, k_ref, v_ref, gr_ref)
        return (yield from _mlstm_chunk(d, qr[rows, :], kr[rows, :], vr[rows, :], grr[n] + bias_ref[...],
                                        tri_ref[d], tri_ref[1 - d], eye_ref[...], ct_ref, n_ref, m_ref, want_out))

    ct_ref[...] = jnp.zeros_like(ct_ref)
    n_ref[...] = jnp.zeros_like(n_ref)
    m_ref[...] = jnp.zeros_like(m_ref)
    return chunk, finish, (g_ref, y_ref), (cg_ref, cy_ref)


def _gate_rows(zs):
    n = zs.shape[0]
    g = zs[:, GLA_LR:].reshape(n // CHUNK, CHUNK, N_DIR * 2, HEADS)
    return jnp.transpose(g, (3, 0, 2, 1))


def _mlstm_specs(t, tc, consts):
    tri, eye = consts
    base = (2 * HEADS * DK + 2 * HEADS * DV)
    bq, bv = base // DK, (base + 2 * HEADS * DK) // DV
    lat = lambda cols, off: pl.BlockSpec((t, cols), lambda b, h, off=off: (b, off + h))
    ctx = lambda cols, off: pl.BlockSpec((tc, cols), lambda b, h, off=off: (b, off + h))
    const = lambda shape: pl.BlockSpec(shape, lambda b, h: (0,) * len(shape))
    return [lat(DK, bq), lat(DK, bq + HEADS), lat(DV, bv), lat(DV, bv + HEADS),
            pl.BlockSpec((None, t // CHUNK, 2 * N_DIR, CHUNK), lambda b, h: (h, b, 0, 0)),
            ctx(DK, bq), ctx(DK, bq + HEADS), ctx(DV, bv), ctx(DV, bv + HEADS),
            pl.BlockSpec((None, tc // CHUNK, 2 * N_DIR, CHUNK), lambda b, h: (h, b, 0, 0)),
            pl.BlockSpec((None, 2 * N_DIR, 1), lambda b, h: (h, 0, 0)),
            const((1, DV)), const(tri.shape), const(eye.shape)]


def _scan_kernel(*refs, with_ctx_out):
    gla_in = refs[:N_GLA_IN]
    ml_in = refs[N_GLA_IN:N_GLA_IN + N_MLSTM_IN]
    rest = refs[N_GLA_IN + N_MLSTM_IN:]
    if with_ctx_out:
        yg_ref, ym_ref, cyg_ref, cym_ref, st_ref, ct_ref, n_ref, m_ref = rest
    else:
        yg_ref, ym_ref, st_ref, ct_ref, n_ref, m_ref = rest
        cyg_ref = cym_ref = None
    streams = [_gla_stream(gla_in, yg_ref, cyg_ref, st_ref),
               _mlstm_stream(ml_in, ym_ref, cym_ref, ct_ref, n_ref, m_ref)]
    _drive_scan(gla_in[5].shape[0] // CHUNK, gla_in[0].shape[0] // CHUNK, with_ctx_out, streams)


def _scans(z, zs, la, cz, czs, cla, gla_g_norm, b_gate, mlstm_g_norm, gla_consts, mlstm_consts, batch,
           with_ctx_out):
    t = z.shape[0] // batch
    tc = cz.shape[0] // batch
    in_specs = _gla_specs(t, tc, gla_consts) + _mlstm_specs(t, tc, mlstm_consts)
    head_out = lambda rows: pl.BlockSpec((rows, DV), lambda b, h: (b, h))
    out_specs = [head_out(t), head_out(t)]
    out_shape = [jax.ShapeDtypeStruct((batch * t, HEADS * DV), BF16)] * 2
    scratch = [pltpu.VMEM((N_DIR, DV, DK), F32), pltpu.VMEM((N_DIR, DV, DK), F32),
               pltpu.VMEM((N_DIR, 1, DK), F32), pltpu.VMEM((N_DIR, 1, DK), F32)]
    if with_ctx_out:
        out_specs += [head_out(tc), head_out(tc)]
        out_shape += [jax.ShapeDtypeStruct((batch * tc, HEADS * DV), BF16)] * 2
    bias = jnp.transpose(b_gate.reshape(2 * N_DIR, HEADS))[:, :, None]
    return pl.pallas_call(
        functools.partial(_scan_kernel, with_ctx_out=with_ctx_out),
        grid=(batch, HEADS),
        in_specs=in_specs,
        out_specs=out_specs,
        out_shape=out_shape,
        scratch_shapes=scratch,
        compiler_params=_params("arbitrary", "arbitrary"),
        name="scans",
    )(z, z, z, z, la, cz, cz, cz, cz, cla, gla_g_norm, *gla_consts,
      z, z, z, z, _gate_rows(zs), cz, cz, cz, cz, _gate_rows(czs), bias, mlstm_g_norm, *mlstm_consts)


def _prep_layer_weights(w_in, w_out, w_up, conv_w, conv_b, w_down):
    hk, hv = HEADS * DK, HEADS * DV
    glr0 = 2 * hk + 2 * hv
    ml0 = glr0 + GLA_LR
    gate0 = ml0 + 2 * hk + 2 * hv
    w_main = jnp.concatenate([w_in[:, :glr0], w_in[:, ml0:gate0]], axis=1).astype(BF16)
    w_small = jnp.concatenate([w_in[:, glr0:ml0], w_in[:, gate0:]], axis=1).astype(BF16)
    pad = D_FF_PAD - D_FF
    zc = jnp.zeros((D_MODEL, pad), w_up.dtype)
    w_up_p = jnp.concatenate([w_up[:, :D_FF], zc, w_up[:, D_FF:], zc], axis=1).astype(BF16)
    conv_w9 = jnp.pad(conv_w.reshape(9, D_FF), ((0, 0), (0, pad)))
    conv_bp = jnp.pad(conv_b.reshape(1, D_FF), ((0, 0), (0, pad)))
    w_down_p = jnp.pad(w_down, ((0, pad), (0, 0))).astype(BF16)
    return w_main, w_small, w_out.astype(BF16), w_up_p, conv_w9, conv_bp, w_down_p


def kernel(x, c, ctx, c_ctx, w_mod, b_mod, g_norm1, g_norm2, w_in, gla_w_lr, gla_b_lr, mlstm_b_gate,
           gla_g_norm, mlstm_g_norm, w_out, w_up, conv_w, conv_b, w_down, g_final):
    batch, seq, _ = x.shape
    ctx_len = ctx.shape[1]
    assert batch < MOD_ROWS and seq == GRID_W * GRID_W and ctx_len % CHUNK == 0
    tm_lat = 512
    tm_ctx = min(512, batch * ctx_len)
    cc = jnp.zeros((MOD_ROWS, D_MODEL), F32).at[:batch].set(c).at[batch].set(c_ctx)
    mod = _modulation(cc, w_mod, b_mod)
    gla_consts = _gla_constants()
    mlstm_consts = _mlstm_constants()

    xl = x.reshape(batch * seq, D_MODEL)
    xc = ctx.reshape(batch * ctx_len, D_MODEL)
    row = lambda v: v.reshape(1, -1)
    hl = _prenorm(xl, row(g_norm1[0]), mod, 0, seq, None, tm_lat)
    hc = _prenorm(xc, row(g_norm1[0]), mod, 0, tm_ctx, batch, tm_ctx)
    out = None
    for l in range(DEPTH):
        last = l == DEPTH - 1
        w_main, w_small, w_out_b, w_up_p, conv_w9, conv_bp, w_down_p = _prep_layer_weights(
            w_in[l], w_out[l], w_up[l], conv_w[l], conv_b[l], w_down[l])
        z, zs = _inproj(hl, w_main, w_small, 1024, 1024)
        cz, czs = _inproj(hc, w_main, w_small, tm_ctx, 1024)
        b_lr = gla_b_lr[l].reshape(N_DIR, 1, HEADS * DK)
        la = _decay(zs, gla_w_lr[l], b_lr, 1024)
        cla = _decay(czs, gla_w_lr[l], b_lr, tm_ctx)
        ys = _scans(z, zs, la, cz, czs, cla, row(gla_g_norm[l]), mlstm_b_gate[l],
                    row(mlstm_g_norm[l]), gla_consts, mlstm_consts, batch, not last)
        x1, h2 = _outproj(ys[0], ys[1], w_out_b, xl, row(g_norm2[l]), mod, l, seq, None, 256)
        act = _upconv(h2, w_up_p, conv_w9, conv_bp, seq, GRID_W, 512)
        if last:
            out = _down(act, w_down_p, x1, mod, l, row(g_final), seq, None, True, tm_lat)
        else:
            xl, hl = _down(act, w_down_p, x1, mod, l, row(g_norm1[l + 1]), seq, None, False, tm_lat)
            c1, hc2 = _outproj(ys[2], ys[3], w_out_b, xc, row(g_norm2[l]), mod, l, 256, batch, 256)
            cact = _upconv(hc2, w_up_p, conv_w9, conv_bp, ctx_len, ctx_len, ctx_len)
            xc, hc = _down(cact, w_down_p, c1, mod, l, row(g_norm1[l + 1]), tm_ctx, batch, False, tm_ctx)
    return out.reshape(batch, seq, D_MODEL)
```

```python
import functools

import numpy as np
import jax
import jax.numpy as jnp
from jax import lax
from jax.experimental import pallas as pl
from jax.experimental.pallas import tpu as pltpu

D_MODEL = 2048
DEPTH = 2
GRID_W = 64
N_DIR = 2
HEADS = 4
DK = 128
DV = 256
GLA_LR = 16
GLA_NORMALIZER = 16.0
CHUNK = 64
D_FF = 5504
D_FF_PAD = 5632
FF_TILE = 512
UPCONV_TILE = 512
EPS = 1e-6
LANE = 128
SCAN_UNROLL = 4
N_LEVELS = 6
N_MOD = 6
MOD_ROWS = 8

VMEM_LIMIT = 56 * 1024 * 1024

F32 = jnp.float32
BF16 = jnp.bfloat16
NN = (((1,), (0,)), ((), ()))
NT = (((1,), (1,)), ((), ()))
TN = (((0,), (0,)), ((), ()))


def _mm(a, b, dims=NN):
    return lax.dot_general(a, b, dims, preferred_element_type=F32)


def _split(x):
    hi = x.astype(BF16)
    return hi, (x - hi.astype(F32)).astype(BF16)


def _mm3(a, b):
    ah, al = _split(a)
    bh, bl = _split(b)
    return _mm(ah, bh) + (_mm(ah, bl) + _mm(al, bh))


def _sigmoid(x):
    return 1.0 / (1.0 + jnp.exp(-x))


def _log_sigmoid(x):
    return jnp.minimum(x, 0.0) - jnp.log(1.0 + jnp.exp(-jnp.abs(x)))


def _rms(x, g):
    return x * lax.rsqrt(jnp.mean(x * x, axis=-1, keepdims=True) + EPS) * g


def _params(*sem):
    return pltpu.CompilerParams(dimension_semantics=sem, vmem_limit_bytes=VMEM_LIMIT)


def _mod_kernel(cc_ref, w_ref, b_ref, o_ref):
    a = cc_ref[...]
    a = a * _sigmoid(a)
    o_ref[...] = _mm3(a, w_ref[...]) + b_ref[...]


def _modulation(cc, w_mod, b_mod):
    tn = 1024
    out = pl.pallas_call(
        _mod_kernel,
        grid=(DEPTH, N_MOD * D_MODEL // tn),
        in_specs=[pl.BlockSpec((MOD_ROWS, D_MODEL), lambda l, j: (0, 0)),
                  pl.BlockSpec((None, D_MODEL, tn), lambda l, j: (l, 0, j)),
                  pl.BlockSpec((None, 1, tn), lambda l, j: (l, 0, j))],
        out_specs=pl.BlockSpec((None, MOD_ROWS, tn), lambda l, j: (l, 0, j)),
        out_shape=jax.ShapeDtypeStruct((DEPTH, MOD_ROWS, N_MOD * D_MODEL), F32),
        compiler_params=_params("arbitrary", "arbitrary"),
        name="modulation",
    )(cc, w_mod, b_mod.reshape(DEPTH, 1, N_MOD * D_MODEL))
    return out.reshape(DEPTH * MOD_ROWS * N_MOD, 1, D_MODEL)


def _mod_spec(layer, piece, seq, ctx_row, tm):
    base = layer * MOD_ROWS * N_MOD + piece
    assert seq % tm == 0
    blocks_per_batch = seq // tm

    def idx(i, *_):
        r = ctx_row if ctx_row is not None else i // blocks_per_batch
        return (base + r * N_MOD, 0, 0)

    return pl.BlockSpec((None, 1, D_MODEL), idx)


def _prenorm_kernel(x_ref, g_ref, sc_ref, sh_ref, o_ref):
    o_ref[...] = (_rms(x_ref[...], g_ref[...]) * (1.0 + sc_ref[...]) + sh_ref[...]).astype(o_ref.dtype)


def _prenorm(x, g, mod, layer, seq, ctx_row, tm):
    n = x.shape[0]
    return pl.pallas_call(
        _prenorm_kernel,
        grid=(n // tm,),
        in_specs=[pl.BlockSpec((tm, D_MODEL), lambda i: (i, 0)),
                  pl.BlockSpec((1, D_MODEL), lambda i: (0, 0)),
                  _mod_spec(layer, 1, seq, ctx_row, tm),
                  _mod_spec(layer, 0, seq, ctx_row, tm)],
        out_specs=pl.BlockSpec((tm, D_MODEL), lambda i: (i, 0)),
        out_shape=jax.ShapeDtypeStruct((n, D_MODEL), BF16),
        compiler_params=_params("arbitrary"),
        name="prenorm",
    )(x, g, mod, mod)


def _inproj_kernel(h_ref, w_ref, ws_ref, z_ref, zs_ref):
    h = h_ref[...]
    z_ref[...] = _mm(h, w_ref[...]).astype(z_ref.dtype)

    @pl.when(pl.program_id(1) == 0)
    def _():
        zs_ref[...] = _mm(h, ws_ref[...])


def _inproj(h, w_main, w_small, tm, tn):
    n = h.shape[0]
    p = w_main.shape[1]
    ps = w_small.shape[1]
    return pl.pallas_call(
        _inproj_kernel,
        grid=(n // tm, p // tn),
        in_specs=[pl.BlockSpec((tm, D_MODEL), lambda i, j: (i, 0)),
                  pl.BlockSpec((D_MODEL, tn), lambda i, j: (0, j)),
                  pl.BlockSpec((D_MODEL, ps), lambda i, j: (0, 0))],
        out_specs=[pl.BlockSpec((tm, tn), lambda i, j: (i, j)),
                   pl.BlockSpec((tm, ps), lambda i, j: (i, 0))],
        out_shape=[jax.ShapeDtypeStruct((n, p), BF16), jax.ShapeDtypeStruct((n, ps), F32)],
        compiler_params=_params("arbitrary", "arbitrary"),
        name="inproj",
    )(h, w_main, w_small)


def _outproj_kernel(yg_ref, ym_ref, wg_ref, wm_ref, x_ref, gt_ref, g2_ref, sc_ref, sh_ref, x1_ref, h2_ref):
    acc = _mm(yg_ref[...], wg_ref[...]) + _mm(ym_ref[...], wm_ref[...])
    x1 = x_ref[...] + gt_ref[...] * acc
    x1_ref[...] = x1
    h2_ref[...] = (_rms(x1, g2_ref[...]) * (1.0 + sc_ref[...]) + sh_ref[...]).astype(h2_ref.dtype)


def _outproj(yg, ym, w_out, x, g2, mod, layer, seq, ctx_row, tm):
    n = x.shape[0]
    half = w_out.shape[0] // 2
    row = lambda i: (i, 0)
    return pl.pallas_call(
        _outproj_kernel,
        grid=(n // tm,),
        in_specs=[pl.BlockSpec((tm, half), row),
                  pl.BlockSpec((tm, half), row),
                  pl.BlockSpec((half, D_MODEL), lambda i: (0, 0)),
                  pl.BlockSpec((half, D_MODEL), lambda i: (1, 0)),
                  pl.BlockSpec((tm, D_MODEL), row),
                  _mod_spec(layer, 2, seq, ctx_row, tm),
                  pl.BlockSpec((1, D_MODEL), lambda i: (0, 0)),
                  _mod_spec(layer, 4, seq, ctx_row, tm),
                  _mod_spec(layer, 3, seq, ctx_row, tm)],
        out_specs=[pl.BlockSpec((tm, D_MODEL), row), pl.BlockSpec((tm, D_MODEL), row)],
        out_shape=[jax.ShapeDtypeStruct((n, D_MODEL), F32), jax.ShapeDtypeStruct((n, D_MODEL), BF16)],
        compiler_params=_params("arbitrary"),
        name="outproj",
    )(yg, ym, w_out, w_out, x, mod, g2, mod, mod)


def _upconv_kernel(h_ref, w_ref, cw_ref, cb_ref, act_ref, ug_ref, uv_ref, *, grid_w, vertical, chunk_rows):
    n_tok, tf = ug_ref.shape
    n_rows = n_tok // grid_w
    rows_per_chunk = chunk_rows // grid_w
    n_chunks = n_tok // chunk_rows
    col = lax.broadcasted_iota(jnp.int32, (grid_w, LANE), 0)

    def project(k):
        rows = slice(k * chunk_rows, (k + 1) * chunk_rows)
        u = _mm(h_ref[rows, :], w_ref[...])
        ug_ref[rows, :] = u[:, :tf].astype(ug_ref.dtype)
        uv_ref[rows, :] = u[:, tf:].astype(uv_ref.dtype)

    def conv(k):
        for c in range(tf // LANE):
            lanes = slice(c * LANE, (c + 1) * LANE)
            cw = cw_ref[:, lanes]
            cb = cb_ref[:, lanes]
            for r in range(k * rows_per_chunk, (k + 1) * rows_per_chunk):
                rows = slice(r * grid_w, (r + 1) * grid_w)
                mid = ug_ref[rows, lanes].astype(F32)
                taps = [cw[3 + j:4 + j] * mid for j in range(3)]
                if vertical and r > 0:
                    up = ug_ref[(r - 1) * grid_w:r * grid_w, lanes].astype(F32)
                    taps = [t + cw[j:j + 1] * up for j, t in enumerate(taps)]
                if vertical and r < n_rows - 1:
                    down = ug_ref[(r + 1) * grid_w:(r + 2) * grid_w, lanes].astype(F32)
                    taps = [t + cw[6 + j:7 + j] * down for j, t in enumerate(taps)]
                left = jnp.where(col != 0, pltpu.roll(taps[0], 1, 0), 0.0)
                right = jnp.where(col != grid_w - 1, pltpu.roll(taps[2], grid_w - 1, 0), 0.0)
                g = taps[1] + left + right + cb
                half_g = 0.5 * g
                silu = half_g * (1.0 + jnp.tanh(half_g))
                act_ref[rows, lanes] = (silu * uv_ref[rows, lanes].astype(F32)).astype(act_ref.dtype)

    project(0)
    for k in range(1, n_chunks):
        project(k)
        conv(k - 1)
    conv(n_chunks - 1)


def _upconv(h, w_up, conv_w9, conv_b, img_tokens, grid_w, vertical, chunk_rows):
    n = h.shape[0]
    tf = UPCONV_TILE
    nf = D_FF_PAD // tf
    return pl.pallas_call(
        functools.partial(_upconv_kernel, grid_w=grid_w, vertical=vertical, chunk_rows=chunk_rows),
        grid=(n // img_tokens, nf),
        in_specs=[pl.BlockSpec((img_tokens, D_MODEL), lambda b, f: (b, 0), pipeline_mode=pl.Buffered(1)),
                  pl.BlockSpec((D_MODEL, 2 * tf), lambda b, f: (0, f)),
                  pl.BlockSpec((9, tf), lambda b, f: (0, f)),
                  pl.BlockSpec((1, tf), lambda b, f: (0, f))],
        out_specs=pl.BlockSpec((img_tokens, tf), lambda b, f: (b, f)),
        out_shape=jax.ShapeDtypeStruct((n, D_FF_PAD), BF16),
        scratch_shapes=[pltpu.VMEM((img_tokens, tf), BF16), pltpu.VMEM((img_tokens, tf), BF16)],
        compiler_params=_params("arbitrary", "arbitrary"),
        name="upconv",
    )(h, w_up, conv_w9, conv_b)


def _down_kernel(*refs, final):
    act_ref, wd_ref, x1_ref, gt_ref, gn_ref = refs[:5]
    if final:
        out_ref, acc_ref = refs[5:]
    else:
        sc_ref, sh_ref, x2_ref, hn_ref, acc_ref = refs[5:]
    f = pl.program_id(1)

    @pl.when(f == 0)
    def _():
        acc_ref[...] = jnp.zeros_like(acc_ref)

    acc_ref[...] += _mm(act_ref[...], wd_ref[...])

    @pl.when(f == pl.num_programs(1) - 1)
    def _():
        x2 = x1_ref[...] + gt_ref[...] * acc_ref[...]
        if final:
            out_ref[...] = _rms(x2, gn_ref[...])
        else:
            x2_ref[...] = x2
            hn_ref[...] = (_rms(x2, gn_ref[...]) * (1.0 + sc_ref[...]) + sh_ref[...]).astype(hn_ref.dtype)


def _down(act, w_down, x1, mod, layer, gn, seq, ctx_row, final, tm):
    n = x1.shape[0]
    tk = FF_TILE
    row = lambda i, f: (i, 0)
    in_specs = [pl.BlockSpec((tm, tk), lambda i, f: (i, f)),
                pl.BlockSpec((tk, D_MODEL), lambda i, f: (f, 0)),
                pl.BlockSpec((tm, D_MODEL), row),
                _mod_spec(layer, 5, seq, ctx_row, tm),
                pl.BlockSpec((1, D_MODEL), lambda i, f: (0, 0))]
    args = [act, w_down, x1, mod, gn]
    if final:
        out_specs = pl.BlockSpec((tm, D_MODEL), row)
        out_shape = jax.ShapeDtypeStruct((n, D_MODEL), F32)
    else:
        in_specs += [_mod_spec(layer + 1, 1, seq, ctx_row, tm),
                     _mod_spec(layer + 1, 0, seq, ctx_row, tm)]
        args += [mod, mod]
        out_specs = [pl.BlockSpec((tm, D_MODEL), row), pl.BlockSpec((tm, D_MODEL), row)]
        out_shape = [jax.ShapeDtypeStruct((n, D_MODEL), F32), jax.ShapeDtypeStruct((n, D_MODEL), BF16)]
    return pl.pallas_call(
        functools.partial(_down_kernel, final=final),
        grid=(n // tm, D_FF_PAD // tk),
        in_specs=in_specs,
        out_specs=out_specs,
        out_shape=out_shape,
        scratch_shapes=[pltpu.VMEM((tm, D_MODEL), F32)],
        compiler_params=_params("arbitrary", "arbitrary"),
        name="down",
    )(*args)


def _decay_kernel(zs_ref, wlr_ref, blr_ref, la_ref):
    glr = zs_ref[:, 0:GLA_LR]
    for d in range(N_DIR):
        la_ref[d] = _log_sigmoid(_mm3(glr, wlr_ref[d]) + blr_ref[d]) * (1.0 / GLA_NORMALIZER)


def _decay(zs, w_lr, b_lr, tm):
    n = zs.shape[0]
    width = w_lr.shape[-1]
    return pl.pallas_call(
        _decay_kernel,
        grid=(n // tm,),
        in_specs=[pl.BlockSpec((tm, zs.shape[1]), lambda i: (i, 0)),
                  pl.BlockSpec(w_lr.shape, lambda i: (0, 0, 0)),
                  pl.BlockSpec(b_lr.shape, lambda i: (0, 0, 0))],
        out_specs=pl.BlockSpec((N_DIR, tm, width), lambda i: (0, i, 0)),
        out_shape=jax.ShapeDtypeStruct((N_DIR, n, width), F32),
        compiler_params=_params("arbitrary"),
        name="decay",
    )(zs, w_lr, b_lr)


def _flip2(a):
    return a[..., ::-1, ::-1]


def _gla_constants():
    c = CHUNK
    m = np.zeros((N_LEVELS + 2, c, c), np.float32)
    later = np.zeros((N_LEVELS, c, 1), np.float32)
    pair = np.zeros((N_LEVELS + 1, c, c), np.float32)
    for lvl in range(N_LEVELS):
        n = c >> lvl
        half = n // 2
        for r in range(c):
            bnd = (r // n) * n + half - 1
            if r % n >= half:
                m[lvl, r, bnd + 1:r + 1] = 1.0
                later[lvl, r] = 1.0
            else:
                m[lvl, r, r + 1:bnd + 1] = 1.0
        blk = np.arange(c) // n
        pos = np.arange(c) % n
        pair[lvl] = (blk[:, None] == blk[None, :]) & (pos[:, None] >= half) & (pos[None, :] < half)
    pair[N_LEVELS] = np.eye(c)
    for r in range(c):
        m[N_LEVELS, r, :r + 1] = 1.0
        m[N_LEVELS + 1, r, r + 1:] = 1.0
    m = np.stack([m, _flip2(m)]).reshape(N_DIR, (N_LEVELS + 2) * c, c)
    later = np.stack([later, later[:, ::-1]])
    later = np.broadcast_to(later, (N_DIR, N_LEVELS, c, DK))
    pair = np.stack([pair, _flip2(pair)])
    return jnp.asarray(m, BF16), jnp.asarray(later, F32), jnp.asarray(pair, F32)


def _mlstm_constants():
    tri = np.tril(np.ones((CHUNK, CHUNK), np.float32))
    return jnp.asarray(np.stack([tri, tri.T]), F32), jnp.asarray(np.eye(CHUNK), F32)


def _drive_scan(n_ctx, n_lat, with_ctx_out, streams):
    def one(stream, d, is_ctx, s, n_total, second_half, want_out):
        chunk, finish, lat_io, ctx_io = stream
        gate_ref, y_ref = ctx_io if is_ctx else lat_io
        n = s if d == 0 else n_total - 1 - s
        start = n * CHUNK if isinstance(n, int) else pl.multiple_of(n * CHUNK, CHUNK)
        rows = pl.ds(start, CHUNK)
        o = yield from chunk(d, is_ctx, n, rows, want_out)
        if not want_out:
            return
        if second_half:
            y_ref[rows, :] = finish(y_ref[rows, :].astype(F32) + o, gate_ref[rows, :])
        else:
            y_ref[rows, :] = o.astype(y_ref.dtype)

    def steps(is_ctx, ss, n_total, second_half, want_out):
        waiting = [[one(stream, d, is_ctx, s, n_total, second_half, want_out)
                    for d in range(N_DIR) for stream in streams] for s in ss]
        live = []
        while live or waiting:
            if waiting:
                live.extend(waiting.pop(0))
            for g in list(live):
                try:
                    next(g)
                except StopIteration:
                    live.remove(g)

    assert n_ctx % 2 == 0 and n_lat % (2 * SCAN_UNROLL) == 0
    for s in range(0, n_ctx, n_ctx // 2):
        steps(True, [s + u for u in range(n_ctx // 2)], n_ctx, s >= n_ctx // 2, with_ctx_out)
    per_half = n_lat // (2 * SCAN_UNROLL)
    for half in range(2):
        def body(k, carry, half=half):
            steps(False, [k * SCAN_UNROLL + u for u in range(SCAN_UNROLL)], n_lat, half == 1, True)
            return carry

        lax.fori_loop(half * per_half, (half + 1) * per_half, body, 0)


def _gla_chunk(d, qc, kc, vc, la, seg_ref, later_ref, pair_ref, st_ref, want_out):
    c = CHUNK
    la_hi, la_lo = _split(la)
    seg = seg_ref[d]
    e_all = jnp.exp(_mm(seg, la_hi) + _mm(seg, la_lo))
    yield
    e_cum = e_all[N_LEVELS * c:(N_LEVELS + 1) * c]
    e_end = e_all[(N_LEVELS + 1) * c:(N_LEVELS + 2) * c]
    last = c - 1 if d == 0 else 0
    decay = e_cum[last:last + 1]
    qf = qc.astype(F32) * (DK ** -0.5)
    kf = kc.astype(F32)
    out = None
    if want_out:
        a = pair_ref[d, N_LEVELS] * _mm(qf.astype(BF16), kc, NT)
        for lvl in range(N_LEVELS):
            e = e_all[lvl * c:(lvl + 1) * c]
            later = later_ref[d, lvl]
            ql = (qf * e * later).astype(BF16)
            kl = (kf * e * (1.0 - later)).astype(BF16)
            a = a + pair_ref[d, lvl] * _mm(ql, kl, NT)
            yield
    st = st_ref[d]
    if want_out:
        out = _mm(a.astype(BF16), vc) + _mm((qf * e_cum).astype(BF16), st.astype(BF16), NT)
    st_ref[d] = st * decay + _mm(vc, (kf * e_end).astype(BF16), TN)
    return out


N_GLA_IN = 14


def _gla_stream(in_refs, y_ref, cy_ref, st_ref):
    (q_ref, k_ref, v_ref, g_ref, la_ref, cq_ref, ck_ref, cv_ref, cg_ref, cla_ref,
     gn_ref, seg_ref, later_ref, pair_ref) = in_refs

    def finish(tot, gate):
        gate = gate.astype(F32)
        return (_rms(tot, gn_ref[...]) * (gate * _sigmoid(gate))).astype(BF16)

    def chunk(d, is_ctx, n, rows, want_out):
        del n
        qr, kr, vr, lr = (cq_ref, ck_ref, cv_ref, cla_ref) if is_ctx else (q_ref, k_ref, v_ref, la_ref)
        return (yield from _gla_chunk(d, qr[rows, :], kr[rows, :], vr[rows, :], lr[d, rows, :],
                                      seg_ref, later_ref, pair_ref, st_ref, want_out))

    st_ref[...] = jnp.zeros_like(st_ref)
    return chunk, finish, (g_ref, y_ref), (cg_ref, cy_ref)


def _gla_specs(t, tc, consts):
    seg, later, pair = consts
    lat = lambda cols, off: pl.BlockSpec((t, cols), lambda b, h, off=off: (b, off + h))
    ctx = lambda cols, off: pl.BlockSpec((tc, cols), lambda b, h, off=off: (b, off + h))
    const = lambda shape: pl.BlockSpec(shape, lambda b, h: (0,) * len(shape))
    return [lat(DK, 0), lat(DK, HEADS), lat(DV, HEADS), lat(DV, 2 * HEADS),
            pl.BlockSpec((N_DIR, t, DK), lambda b, h: (0, b, h)),
            ctx(DK, 0), ctx(DK, HEADS), ctx(DV, HEADS), ctx(DV, 2 * HEADS),
            pl.BlockSpec((N_DIR, tc, DK), lambda b, h: (0, b, h)),
            const((1, DV)), const(seg.shape), const(later.shape), const(pair.shape)]


def _mlstm_chunk(d, qc, kc, vc, gates, tri, tri_t, eye, ct_ref, n_ref, m_ref, want_out):
    li_row = gates[2 * d:2 * d + 1]
    lf_row = _log_sigmoid(gates[2 * d + 1:2 * d + 2])
    li_col = jnp.sum(eye * li_row, axis=1, keepdims=True)
    lf_col = jnp.sum(eye * lf_row, axis=1, keepdims=True)
    fcum_col = jnp.sum(tri * lf_row, axis=1, keepdims=True)
    fcum_row = jnp.sum(tri_t * lf_col, axis=0, keepdims=True)
    f_end = jnp.sum(lf_row, axis=1, keepdims=True)
    yield
    m = m_ref[d][:, 0:1]
    m_new = jnp.maximum(f_end + m, jnp.max(f_end - fcum_row + li_row, axis=1, keepdims=True))
    carry = jnp.exp(f_end + m - m_new)
    w_col = jnp.exp(f_end - fcum_col + li_col - m_new)
    m_ref[d] = jnp.broadcast_to(m_new, m_ref.shape[1:])
    kf = kc.astype(F32) * (DK ** -0.5)
    kw = kf * w_col
    yield
    out = None
    if want_out:
        qf = qc.astype(F32)
        d_log = fcum_col - fcum_row + li_row
        causal = tri > 0.5
        inter_log = fcum_col + m
        m_q = jnp.maximum(inter_log, jnp.max(jnp.where(causal, d_log, -jnp.inf), axis=1, keepdims=True))
        p = jnp.where(causal, jnp.exp(d_log - m_q), 0.0)
        s = _mm(qc, kf.astype(BF16), NT) * p
        inter = jnp.exp(inter_log - m_q)
        yield
    ct = ct_ref[d]
    nvec = n_ref[d]
    if want_out:
        num = _mm(s.astype(BF16), vc) + inter * _mm(qc, ct.astype(BF16), NT)
        den = jnp.sum(s, axis=1, keepdims=True) + inter * jnp.sum(qf * nvec, axis=1, keepdims=True)
        out = num / jnp.maximum(jnp.abs(den), jnp.exp(-m_q))
    ct_ref[d] = carry * ct + _mm(vc, kw.astype(BF16), TN)
    n_ref[d] = carry * nvec + jnp.sum(kw, axis=0, keepdims=True)
    return out


N_MLSTM_IN = 14


def _mlstm_stream(in_refs, y_ref, cy_ref, ct_ref, n_ref, m_ref):
    (q_ref, k_ref, v_ref, g_ref, gr_ref, cq_ref, ck_ref, cv_ref, cg_ref, cgr_ref,
     bias_ref, gn_ref, tri_ref, eye_ref) = in_refs

    def finish(tot, gate):
        return (_sigmoid(gate.astype(F32)) * _rms(tot, gn_ref[...])).astype(BF16)

    def chunk(d, is_ctx, n, rows, want_out):
        qr, kr, vr, grr = (cq_ref, ck_ref, cv_ref, cgr_ref) if is_ctx else (q_ref, k_ref, v_ref, gr_ref)
        return (yield from _mlstm_chunk(d, qr[rows, :], kr[rows, :], vr[rows, :], grr[n] + bias_ref[...],
                                        tri_ref[d], tri_ref[1 - d], eye_ref[...], ct_ref, n_ref, m_ref, want_out))

    ct_ref[...] = jnp.zeros_like(ct_ref)
    n_ref[...] = jnp.zeros_like(n_ref)
    m_ref[...] = jnp.zeros_like(m_ref)
    return chunk, finish, (g_ref, y_ref), (cg_ref, cy_ref)


def _gate_rows(zs):
    n = zs.shape[0]
    g = zs[:, GLA_LR:].reshape(n // CHUNK, CHUNK, N_DIR * 2, HEADS)
    return jnp.transpose(g, (3, 0, 2, 1))


def _mlstm_specs(t, tc, consts):
    tri, eye = consts
    base = (2 * HEADS * DK + 2 * HEADS * DV)
    bq, bv = base // DK, (base + 2 * HEADS * DK) // DV
    lat = lambda cols, off: pl.BlockSpec((t, cols), lambda b, h, off=off: (b, off + h))
    ctx = lambda cols, off: pl.BlockSpec((tc, cols), lambda b, h, off=off: (b, off + h))
    const = lambda shape: pl.BlockSpec(shape, lambda b, h: (0,) * len(shape))
    return [lat(DK, bq), lat(DK, bq + HEADS), lat(DV, bv), lat(DV, bv + HEADS),
            pl.BlockSpec((None, t // CHUNK, 2 * N_DIR, CHUNK), lambda b, h: (h, b, 0, 0)),
            ctx(DK, bq), ctx(DK, bq + HEADS), ctx(DV, bv), ctx(DV, bv + HEADS),
            pl.BlockSpec((None, tc // CHUNK, 2 * N_DIR, CHUNK), lambda b, h: (h, b, 0, 0)),
            pl.BlockSpec((None, 2 * N_DIR, 1), lambda b, h: (h, 0, 0)),
            const((1, DV)), const(tri.shape), const(eye.shape)]


def _scan_kernel(*refs, with_ctx_out):
    gla_in = refs[:N_GLA_IN]
    ml_in = refs[N_GLA_IN:N_GLA_IN + N_MLSTM_IN]
    rest = refs[N_GLA_IN + N_MLSTM_IN:]
    if with_ctx_out:
        yg_ref, ym_ref, cyg_ref, cym_ref, st_ref, ct_ref, n_ref, m_ref = rest
    else:
        yg_ref, ym_ref, st_ref, ct_ref, n_ref, m_ref = rest
        cyg_ref = cym_ref = None
    streams = [_gla_stream(gla_in, yg_ref, cyg_ref, st_ref),
               _mlstm_stream(ml_in, ym_ref, cym_ref, ct_ref, n_ref, m_ref)]
    _drive_scan(gla_in[5].shape[0] // CHUNK, gla_in[0].shape[0] // CHUNK, with_ctx_out, streams)


def _scans(z, zs, la, cz, czs, cla, gla_g_norm, b_gate, mlstm_g_norm, gla_consts, mlstm_consts, batch,
           with_ctx_out):
    t = z.shape[0] // batch
    tc = cz.shape[0] // batch
    in_specs = _gla_specs(t, tc, gla_consts) + _mlstm_specs(t, tc, mlstm_consts)
    head_out = lambda rows: pl.BlockSpec((rows, DV), lambda b, h: (b, h))
    out_specs = [head_out(t), head_out(t)]
    out_shape = [jax.ShapeDtypeStruct((batch * t, HEADS * DV), BF16)] * 2
    scratch = [pltpu.VMEM((N_DIR, DV, DK), F32), pltpu.VMEM((N_DIR, DV, DK), F32),
               pltpu.VMEM((N_DIR, 1, DK), F32), pltpu.VMEM((N_DIR, 1, DK), F32)]
    if with_ctx_out:
        out_specs += [head_out(tc), head_out(tc)]
        out_shape += [jax.ShapeDtypeStruct((batch * tc, HEADS * DV), BF16)] * 2
    bias = jnp.transpose(b_gate.reshape(2 * N_DIR, HEADS))[:, :, None]
    return pl.pallas_call(
        functools.partial(_scan_kernel, with_ctx_out=with_ctx_out),
        grid=(batch, HEADS),
        in_specs=in_specs,
        out_specs=out_specs,
        out_shape=out_shape,
        scratch_shapes=scratch,
        compiler_params=_params("arbitrary", "arbitrary"),
        name="scans",
    )(z, z, z, z, la, cz, cz, cz, cz, cla, gla_g_norm, *gla_consts,
      z, z, z, z, _gate_rows(zs), cz, cz, cz, cz, _gate_rows(czs), bias, mlstm_g_norm, *mlstm_consts)


def _prep_layer_weights(w_in, w_out, w_up, conv_w, conv_b, w_down):
    hk, hv = HEADS * DK, HEADS * DV
    glr0 = 2 * hk + 2 * hv
    ml0 = glr0 + GLA_LR
    gate0 = ml0 + 2 * hk + 2 * hv
    w_main = jnp.concatenate([w_in[:, :glr0], w_in[:, ml0:gate0]], axis=1).astype(BF16)
    w_small = jnp.concatenate([w_in[:, glr0:ml0], w_in[:, gate0:]], axis=1).astype(BF16)
    pad = D_FF_PAD - D_FF
    zc = jnp.zeros((D_MODEL, pad), w_up.dtype)
    halves = [jnp.concatenate([w, zc], axis=1).reshape(D_MODEL, D_FF_PAD // UPCONV_TILE, UPCONV_TILE)
              for w in (w_up[:, :D_FF], w_up[:, D_FF:])]
    w_up_p = jnp.stack(halves, axis=2).reshape(D_MODEL, 2 * D_FF_PAD).astype(BF16)
    conv_w9 = jnp.pad(conv_w.reshape(9, D_FF), ((0, 0), (0, pad)))
    conv_bp = jnp.pad(conv_b.reshape(1, D_FF), ((0, 0), (0, pad)))
    w_down_p = jnp.pad(w_down, ((0, pad), (0, 0))).astype(BF16)
    return w_main, w_small, w_out.astype(BF16), w_up_p, conv_w9, conv_bp, w_down_p


def kernel(x, c, ctx, c_ctx, w_mod, b_mod, g_norm1, g_norm2, w_in, gla_w_lr, gla_b_lr, mlstm_b_gate,
           gla_g_norm, mlstm_g_norm, w_out, w_up, conv_w, conv_b, w_down, g_final):
    batch, seq, _ = x.shape
    ctx_len = ctx.shape[1]
    assert batch < MOD_ROWS and seq == GRID_W * GRID_W and ctx_len % CHUNK == 0
    tm_lat = 512
    tm_ctx = min(512, batch * ctx_len)
    cc = jnp.zeros((MOD_ROWS, D_MODEL), F32).at[:batch].set(c).at[batch].set(c_ctx)
    mod = _modulation(cc, w_mod, b_mod)
    gla_consts = _gla_constants()
    mlstm_consts = _mlstm_constants()

    xl = x.reshape(batch * seq, D_MODEL)
    xc = ctx.reshape(batch * ctx_len, D_MODEL)
    row = lambda v: v.reshape(1, -1)
    hl = _prenorm(xl, row(g_norm1[0]), mod, 0, seq, None, tm_lat)
    hc = _prenorm(xc, row(g_norm1[0]), mod, 0, tm_ctx, batch, tm_ctx)
    out = None
    for l in range(DEPTH):
        last = l == DEPTH - 1
        w_main, w_small, w_out_b, w_up_p, conv_w9, conv_bp, w_down_p = _prep_layer_weights(
            w_in[l], w_out[l], w_up[l], conv_w[l], conv_b[l], w_down[l])
        z, zs = _inproj(hl, w_main, w_small, 1024, 1024)
        cz, czs = _inproj(hc, w_main, w_small, tm_ctx, 1024)
        b_lr = gla_b_lr[l].reshape(N_DIR, 1, HEADS * DK)
        la = _decay(zs, gla_w_lr[l], b_lr, 1024)
        cla = _decay(czs, gla_w_lr[l], b_lr, tm_ctx)
        ys = _scans(z, zs, la, cz, czs, cla, row(gla_g_norm[l]), mlstm_b_gate[l],
                    row(mlstm_g_norm[l]), gla_consts, mlstm_consts, batch, not last)
        x1, h2 = _outproj(ys[0], ys[1], w_out_b, xl, row(g_norm2[l]), mod, l, seq, None, 256)
        act = _upconv(h2, w_up_p, conv_w9, conv_bp, seq, GRID_W, True, 256)
        if last:
            out = _down(act, w_down_p, x1, mod, l, row(g_final), seq, None, True, tm_lat)
        else:
            xl, hl = _down(act, w_down_p, x1, mod, l, row(g_norm1[l + 1]), seq, None, False, tm_lat)
            c1, hc2 = _outproj(ys[2], ys[3], w_out_b, xc, row(g_norm2[l]), mod, l, 256, batch, 256)
            cact = _upconv(hc2, w_up_p, conv_w9, conv_bp, batch * ctx_len, ctx_len, False, ctx_len)
            xc, hc = _down(cact, w_down_p, c1, mod, l, row(g_norm1[l + 1]), tm_ctx, batch, False, tm_ctx)
    return out.reshape(batch, seq, D_MODEL)
```

```python
import functools

import numpy as np
import jax
import jax.numpy as jnp
from jax import lax
from jax.experimental import pallas as pl
from jax.experimental.pallas import tpu as pltpu

D_MODEL = 2048
DEPTH = 2
GRID_W = 64
N_DIR = 2
HEADS = 4
DK = 128
DV = 256
GLA_LR = 16
GLA_NORMALIZER = 16.0
CHUNK = 64
D_FF = 5504
D_FF_PAD = 5632
FF_TILE = 512
UPCONV_TILE = 512
EPS = 1e-6
LANE = 128
SCAN_UNROLL = 4
N_LEVELS = 6
N_MOD = 6
MOD_ROWS = 8

VMEM_LIMIT = 56 * 1024 * 1024

F32 = jnp.float32
BF16 = jnp.bfloat16
NN = (((1,), (0,)), ((), ()))
NT = (((1,), (1,)), ((), ()))
TN = (((0,), (0,)), ((), ()))


def _mm(a, b, dims=NN):
    return lax.dot_general(a, b, dims, preferred_element_type=F32)


def _split(x):
    hi = x.astype(BF16)
    return hi, (x - hi.astype(F32)).astype(BF16)


def _mm3(a, b):
    ah, al = _split(a)
    bh, bl = _split(b)
    return _mm(ah, bh) + (_mm(ah, bl) + _mm(al, bh))


def _sigmoid(x):
    return 1.0 / (1.0 + jnp.exp(-x))


def _log_sigmoid(x):
    return jnp.minimum(x, 0.0) - jnp.log(1.0 + jnp.exp(-jnp.abs(x)))


def _rms(x, g):
    return x * lax.rsqrt(jnp.mean(x * x, axis=-1, keepdims=True) + EPS) * g


def _params(*sem):
    return pltpu.CompilerParams(dimension_semantics=sem, vmem_limit_bytes=VMEM_LIMIT)


def _mod_kernel(cc_ref, w_ref, b_ref, o_ref):
    a = cc_ref[...]
    a = a * _sigmoid(a)
    o_ref[...] = _mm3(a, w_ref[...]) + b_ref[...]


def _modulation(cc, w_mod, b_mod):
    tn = 1024
    out = pl.pallas_call(
        _mod_kernel,
        grid=(DEPTH, N_MOD * D_MODEL // tn),
        in_specs=[pl.BlockSpec((MOD_ROWS, D_MODEL), lambda l, j: (0, 0)),
                  pl.BlockSpec((None, D_MODEL, tn), lambda l, j: (l, 0, j)),
                  pl.BlockSpec((None, 1, tn), lambda l, j: (l, 0, j))],
        out_specs=pl.BlockSpec((None, MOD_ROWS, tn), lambda l, j: (l, 0, j)),
        out_shape=jax.ShapeDtypeStruct((DEPTH, MOD_ROWS, N_MOD * D_MODEL), F32),
        compiler_params=_params("arbitrary", "arbitrary"),
        name="modulation",
    )(cc, w_mod, b_mod.reshape(DEPTH, 1, N_MOD * D_MODEL))
    return out.reshape(DEPTH * MOD_ROWS * N_MOD, 1, D_MODEL)


def _mod_spec(layer, piece, seq, ctx_row, tm):
    base = layer * MOD_ROWS * N_MOD + piece
    assert seq % tm == 0
    blocks_per_batch = seq // tm

    def idx(i, *_):
        r = ctx_row if ctx_row is not None else i // blocks_per_batch
        return (base + r * N_MOD, 0, 0)

    return pl.BlockSpec((None, 1, D_MODEL), idx)


def _inproj_kernel(h_ref, w_ref, ws_ref, z_ref, zs_ref):
    h = h_ref[...]
    z_ref[...] = _mm(h, w_ref[...]).astype(z_ref.dtype)

    @pl.when(pl.program_id(1) == 0)
    def _():
        zs_ref[...] = _mm(h, ws_ref[...])


def _inproj(h, w_main, w_small, tm, tn):
    n = h.shape[0]
    p = w_main.shape[1]
    ps = w_small.shape[1]
    return pl.pallas_call(
        _inproj_kernel,
        grid=(n // tm, p // tn),
        in_specs=[pl.BlockSpec((tm, D_MODEL), lambda i, j: (i, 0)),
                  pl.BlockSpec((D_MODEL, tn), lambda i, j: (0, j)),
                  pl.BlockSpec((D_MODEL, ps), lambda i, j: (0, 0))],
        out_specs=[pl.BlockSpec((tm, tn), lambda i, j: (i, j)),
                   pl.BlockSpec((tm, ps), lambda i, j: (i, 0))],
        out_shape=[jax.ShapeDtypeStruct((n, p), BF16), jax.ShapeDtypeStruct((n, ps), F32)],
        compiler_params=_params("arbitrary", "arbitrary"),
        name="inproj",
    )(h, w_main, w_small)


def _norm_inproj_kernel(x_ref, g_ref, sc_ref, sh_ref, w_ref, ws_ref, z_ref, zs_ref, h_ref):
    @pl.when(pl.program_id(1) == 0)
    def _():
        h = (_rms(x_ref[...], g_ref[...]) * (1.0 + sc_ref[...]) + sh_ref[...]).astype(h_ref.dtype)
        h_ref[...] = h
        zs_ref[...] = _mm(h, ws_ref[...])

    z_ref[...] = _mm(h_ref[...], w_ref[...]).astype(z_ref.dtype)


def _norm_inproj(x, g, mod, layer, seq, ctx_row, w_main, w_small, tm, tn):
    n = x.shape[0]
    p = w_main.shape[1]
    ps = w_small.shape[1]
    return pl.pallas_call(
        _norm_inproj_kernel,
        grid=(n // tm, p // tn),
        in_specs=[pl.BlockSpec((tm, D_MODEL), lambda i, j: (i, 0)),
                  pl.BlockSpec((1, D_MODEL), lambda i, j: (0, 0)),
                  _mod_spec(layer, 1, seq, ctx_row, tm),
                  _mod_spec(layer, 0, seq, ctx_row, tm),
                  pl.BlockSpec((D_MODEL, tn), lambda i, j: (0, j)),
                  pl.BlockSpec((D_MODEL, ps), lambda i, j: (0, 0))],
        out_specs=[pl.BlockSpec((tm, tn), lambda i, j: (i, j)),
                   pl.BlockSpec((tm, ps), lambda i, j: (i, 0))],
        out_shape=[jax.ShapeDtypeStruct((n, p), BF16), jax.ShapeDtypeStruct((n, ps), F32)],
        scratch_shapes=[pltpu.VMEM((tm, D_MODEL), BF16)],
        compiler_params=_params("arbitrary", "arbitrary"),
        name="norm_inproj",
    )(x, g, mod, mod, w_main, w_small)


def _outproj_kernel(yg_ref, ym_ref, wg_ref, wm_ref, x_ref, gt_ref, g2_ref, sc_ref, sh_ref, x1_ref, h2_ref):
    acc = _mm(yg_ref[...], wg_ref[...]) + _mm(ym_ref[...], wm_ref[...])
    x1 = x_ref[...] + gt_ref[...] * acc
    x1_ref[...] = x1
    h2_ref[...] = (_rms(x1, g2_ref[...]) * (1.0 + sc_ref[...]) + sh_ref[...]).astype(h2_ref.dtype)


def _outproj(yg, ym, w_out, x, g2, mod, layer, seq, ctx_row, tm):
    n = x.shape[0]
    half = w_out.shape[0] // 2
    row = lambda i: (i, 0)
    return pl.pallas_call(
        _outproj_kernel,
        grid=(n // tm,),
        in_specs=[pl.BlockSpec((tm, half), row),
                  pl.BlockSpec((tm, half), row),
                  pl.BlockSpec((half, D_MODEL), lambda i: (0, 0)),
                  pl.BlockSpec((half, D_MODEL), lambda i: (1, 0)),
                  pl.BlockSpec((tm, D_MODEL), row),
                  _mod_spec(layer, 2, seq, ctx_row, tm),
                  pl.BlockSpec((1, D_MODEL), lambda i: (0, 0)),
                  _mod_spec(layer, 4, seq, ctx_row, tm),
                  _mod_spec(layer, 3, seq, ctx_row, tm)],
        out_specs=[pl.BlockSpec((tm, D_MODEL), row), pl.BlockSpec((tm, D_MODEL), row)],
        out_shape=[jax.ShapeDtypeStruct((n, D_MODEL), F32), jax.ShapeDtypeStruct((n, D_MODEL), BF16)],
        compiler_params=_params("arbitrary"),
        name="outproj",
    )(yg, ym, w_out, w_out, x, mod, g2, mod, mod)


def _upconv_kernel(h_ref, wg_ref, wv_ref, cw_ref, cb_ref, act_ref, ug_ref, uv_ref, *, grid_w, vertical, chunk_rows):
    n_tok, tf = ug_ref.shape
    n_rows = n_tok // grid_w
    rows_per_chunk = chunk_rows // grid_w
    n_chunks = n_tok // chunk_rows
    col = lax.broadcasted_iota(jnp.int32, (grid_w, LANE), 0)

    def project(k):
        rows = slice(k * chunk_rows, (k + 1) * chunk_rows)
        ug_ref[rows, :] = _mm(h_ref[rows, :], wg_ref[...]).astype(ug_ref.dtype)
        uv_ref[rows, :] = _mm(h_ref[rows, :], wv_ref[...]).astype(uv_ref.dtype)

    def conv(k):
        for c in range(tf // LANE):
            lanes = slice(c * LANE, (c + 1) * LANE)
            cw = cw_ref[:, lanes]
            cb = cb_ref[:, lanes]
            for r in range(k * rows_per_chunk, (k + 1) * rows_per_chunk):
                rows = slice(r * grid_w, (r + 1) * grid_w)
                mid = ug_ref[rows, lanes].astype(F32)
                taps = [cw[3 + j:4 + j] * mid for j in range(3)]
                if vertical and r > 0:
                    up = ug_ref[(r - 1) * grid_w:r * grid_w, lanes].astype(F32)
                    taps = [t + cw[j:j + 1] * up for j, t in enumerate(taps)]
                if vertical and r < n_rows - 1:
                    down = ug_ref[(r + 1) * grid_w:(r + 2) * grid_w, lanes].astype(F32)
                    taps = [t + cw[6 + j:7 + j] * down for j, t in enumerate(taps)]
                left = jnp.where(col != 0, pltpu.roll(taps[0], 1, 0), 0.0)
                right = jnp.where(col != grid_w - 1, pltpu.roll(taps[2], grid_w - 1, 0), 0.0)
                g = taps[1] + left + right + cb
                half_g = 0.5 * g
                silu = half_g * (1.0 + jnp.tanh(half_g))
                act_ref[rows, lanes] = (silu * uv_ref[rows, lanes].astype(F32)).astype(act_ref.dtype)

    project(0)
    for k in range(1, n_chunks):
        project(k)
        conv(k - 1)
    conv(n_chunks - 1)


def _upconv(h, w_up, conv_w9, conv_b, img_tokens, grid_w, vertical, chunk_rows):
    n = h.shape[0]
    tf = UPCONV_TILE
    nf = D_FF_PAD // tf
    return pl.pallas_call(
        functools.partial(_upconv_kernel, grid_w=grid_w, vertical=vertical, chunk_rows=chunk_rows),
        grid=(n // img_tokens, nf),
        in_specs=[pl.BlockSpec((img_tokens, D_MODEL), lambda b, f: (b, 0), pipeline_mode=pl.Buffered(1)),
                  pl.BlockSpec((D_MODEL, tf), lambda b, f: (0, f)),
                  pl.BlockSpec((D_MODEL, tf), lambda b, f: (0, nf + f)),
                  pl.BlockSpec((9, tf), lambda b, f: (0, f)),
                  pl.BlockSpec((1, tf), lambda b, f: (0, f))],
        out_specs=pl.BlockSpec((img_tokens, tf), lambda b, f: (b, f)),
        out_shape=jax.ShapeDtypeStruct((n, D_FF_PAD), BF16),
        scratch_shapes=[pltpu.VMEM((img_tokens, tf), BF16), pltpu.VMEM((img_tokens, tf), BF16)],
        compiler_params=_params("arbitrary", "arbitrary"),
        name="upconv",
    )(h, w_up, w_up, conv_w9, conv_b)


def _down_kernel(*refs, final):
    act_ref, wd_ref, x1_ref, gt_ref, gn_ref = refs[:5]
    if final:
        out_ref, acc_ref = refs[5:]
    else:
        sc_ref, sh_ref, x2_ref, hn_ref, acc_ref = refs[5:]
    f = pl.program_id(1)

    @pl.when(f == 0)
    def _():
        acc_ref[...] = jnp.zeros_like(acc_ref)

    acc_ref[...] += _mm(act_ref[...], wd_ref[...])

    @pl.when(f == pl.num_programs(1) - 1)
    def _():
        x2 = x1_ref[...] + gt_ref[...] * acc_ref[...]
        if final:
            out_ref[...] = _rms(x2, gn_ref[...])
        else:
            x2_ref[...] = x2
            hn_ref[...] = (_rms(x2, gn_ref[...]) * (1.0 + sc_ref[...]) + sh_ref[...]).astype(hn_ref.dtype)


def _down(act, w_down, x1, mod, layer, gn, seq, ctx_row, final, tm):
    n = x1.shape[0]
    tk = FF_TILE
    row = lambda i, f: (i, 0)
    in_specs = [pl.BlockSpec((tm, tk), lambda i, f: (i, f)),
                pl.BlockSpec((tk, D_MODEL), lambda i, f: (f, 0)),
                pl.BlockSpec((tm, D_MODEL), row),
                _mod_spec(layer, 5, seq, ctx_row, tm),
                pl.BlockSpec((1, D_MODEL), lambda i, f: (0, 0))]
    args = [act, w_down, x1, mod, gn]
    if final:
        out_specs = pl.BlockSpec((tm, D_MODEL), row)
        out_shape = jax.ShapeDtypeStruct((n, D_MODEL), F32)
    else:
        in_specs += [_mod_spec(layer + 1, 1, seq, ctx_row, tm),
                     _mod_spec(layer + 1, 0, seq, ctx_row, tm)]
        args += [mod, mod]
        out_specs = [pl.BlockSpec((tm, D_MODEL), row), pl.BlockSpec((tm, D_MODEL), row)]
        out_shape = [jax.ShapeDtypeStruct((n, D_MODEL), F32), jax.ShapeDtypeStruct((n, D_MODEL), BF16)]
    return pl.pallas_call(
        functools.partial(_down_kernel, final=final),
        grid=(n // tm, D_FF_PAD // tk),
        in_specs=in_specs,
        out_specs=out_specs,
        out_shape=out_shape,
        scratch_shapes=[pltpu.VMEM((tm, D_MODEL), F32)],
        compiler_params=_params("arbitrary", "arbitrary"),
        name="down",
    )(*args)


def _decay_kernel(zs_ref, wlr_ref, blr_ref, la_ref):
    glr = zs_ref[:, 0:GLA_LR]
    for d in range(N_DIR):
        la_ref[d] = _log_sigmoid(_mm3(glr, wlr_ref[d]) + blr_ref[d]) * (1.0 / GLA_NORMALIZER)


def _decay(zs, w_lr, b_lr, tm):
    n = zs.shape[0]
    width = w_lr.shape[-1]
    return pl.pallas_call(
        _decay_kernel,
        grid=(n // tm,),
        in_specs=[pl.BlockSpec((tm, zs.shape[1]), lambda i: (i, 0)),
                  pl.BlockSpec(w_lr.shape, lambda i: (0, 0, 0)),
                  pl.BlockSpec(b_lr.shape, lambda i: (0, 0, 0))],
        out_specs=pl.BlockSpec((N_DIR, tm, width), lambda i: (0, i, 0)),
        out_shape=jax.ShapeDtypeStruct((N_DIR, n, width), F32),
        compiler_params=_params("arbitrary"),
        name="decay",
    )(zs, w_lr, b_lr)


def _flip2(a):
    return a[..., ::-1, ::-1]


def _gla_constants():
    c = CHUNK
    m = np.zeros((N_LEVELS + 2, c, c), np.float32)
    later = np.zeros((N_LEVELS, c, 1), np.float32)
    pair = np.zeros((N_LEVELS + 1, c, c), np.float32)
    for lvl in range(N_LEVELS):
        n = c >> lvl
        half = n // 2
        for r in range(c):
            bnd = (r // n) * n + half - 1
            if r % n >= half:
                m[lvl, r, bnd + 1:r + 1] = 1.0
                later[lvl, r] = 1.0
            else:
                m[lvl, r, r + 1:bnd + 1] = 1.0
        blk = np.arange(c) // n
        pos = np.arange(c) % n
        pair[lvl] = (blk[:, None] == blk[None, :]) & (pos[:, None] >= half) & (pos[None, :] < half)
    pair[N_LEVELS] = np.eye(c)
    for r in range(c):
        m[N_LEVELS, r, :r + 1] = 1.0
        m[N_LEVELS + 1, r, r + 1:] = 1.0
    m = np.stack([m, _flip2(m)]).reshape(N_DIR, (N_LEVELS + 2) * c, c)
    later = np.stack([later, later[:, ::-1]])
    later = np.broadcast_to(later, (N_DIR, N_LEVELS, c, DK))
    pair = np.stack([pair, _flip2(pair)])
    return jnp.asarray(m, BF16), jnp.asarray(later, F32), jnp.asarray(pair, F32)


def _mlstm_constants():
    tri = np.tril(np.ones((CHUNK, CHUNK), np.float32))
    return jnp.asarray(np.stack([tri, tri.T]), F32), jnp.asarray(np.eye(CHUNK), F32)


def _drive_scan(n_ctx, n_lat, with_ctx_out, streams):
    def one(stream, d, is_ctx, s, n_total, second_half, want_out):
        chunk, finish, lat_io, ctx_io = stream
        gate_ref, y_ref = ctx_io if is_ctx else lat_io
        n = s if d == 0 else n_total - 1 - s
        start = n * CHUNK if isinstance(n, int) else pl.multiple_of(n * CHUNK, CHUNK)
        rows = pl.ds(start, CHUNK)
        o = yield from chunk(d, is_ctx, n, rows, want_out)
        if not want_out:
            return
        if second_half:
            y_ref[rows, :] = finish(y_ref[rows, :].astype(F32) + o, gate_ref[rows, :])
        else:
            y_ref[rows, :] = o.astype(y_ref.dtype)

    def steps(is_ctx, ss, n_total, second_half, want_out):
        waiting = [[one(stream, d, is_ctx, s, n_total, second_half, want_out)
                    for d in range(N_DIR) for stream in streams] for s in ss]
        live = []
        while live or waiting:
            if waiting:
                live.extend(waiting.pop(0))
            for g in list(live):
                try:
                    next(g)
                except StopIteration:
                    live.remove(g)

    assert n_ctx % 2 == 0 and n_lat % (2 * SCAN_UNROLL) == 0
    for s in range(0, n_ctx, n_ctx // 2):
        steps(True, [s + u for u in range(n_ctx // 2)], n_ctx, s >= n_ctx // 2, with_ctx_out)
    per_half = n_lat // (2 * SCAN_UNROLL)
    for half in range(2):
        def body(k, carry, half=half):
            steps(False, [k * SCAN_UNROLL + u for u in range(SCAN_UNROLL)], n_lat, half == 1, True)
            return carry

        lax.fori_loop(half * per_half, (half + 1) * per_half, body, 0)


def _gla_chunk(d, qc, kc, vc, la, seg_ref, later_ref, pair_ref, st_ref, want_out):
    c = CHUNK
    la_hi, la_lo = _split(la)
    seg = seg_ref[d]
    e_all = jnp.exp(_mm(seg, la_hi) + _mm(seg, la_lo))
    yield
    e_cum = e_all[N_LEVELS * c:(N_LEVELS + 1) * c]
    e_end = e_all[(N_LEVELS + 1) * c:(N_LEVELS + 2) * c]
    last = c - 1 if d == 0 else 0
    decay = e_cum[last:last + 1]
    qf = qc.astype(F32) * (DK ** -0.5)
    kf = kc.astype(F32)
    out = None
    if want_out:
        a = pair_ref[d, N_LEVELS] * _mm(qf.astype(BF16), kc, NT)
        for lvl in range(N_LEVELS):
            e = e_all[lvl * c:(lvl + 1) * c]
            later = later_ref[d, lvl]
            ql = (qf * e * later).astype(BF16)
            kl = (kf * e * (1.0 - later)).astype(BF16)
            a = a + pair_ref[d, lvl] * _mm(ql, kl, NT)
            yield
    st = st_ref[d]
    if want_out:
        out = _mm(a.astype(BF16), vc) + _mm((qf * e_cum).astype(BF16), st.astype(BF16), NT)
    st_ref[d] = st * decay + _mm(vc, (kf * e_end).astype(BF16), TN)
    return out


N_GLA_IN = 14


def _gla_stream(in_refs, y_ref, cy_ref, st_ref):
    (q_ref, k_ref, v_ref, g_ref, la_ref, cq_ref, ck_ref, cv_ref, cg_ref, cla_ref,
     gn_ref, seg_ref, later_ref, pair_ref) = in_refs

    def finish(tot, gate):
        gate = gate.astype(F32)
        return (_rms(tot, gn_ref[...]) * (gate * _sigmoid(gate))).astype(BF16)

    def chunk(d, is_ctx, n, rows, want_out):
        del n
        qr, kr, vr, lr = (cq_ref, ck_ref, cv_ref, cla_ref) if is_ctx else (q_ref, k_ref, v_ref, la_ref)
        return (yield from _gla_chunk(d, qr[rows, :], kr[rows, :], vr[rows, :], lr[d, rows, :],
                                      seg_ref, later_ref, pair_ref, st_ref, want_out))

    st_ref[...] = jnp.zeros_like(st_ref)
    return chunk, finish, (g_ref, y_ref), (cg_ref, cy_ref)


def _gla_specs(t, tc, consts):
    seg, later, pair = consts
    lat = lambda cols, off: pl.BlockSpec((t, cols), lambda b, h, off=off: (b, off + h))
    ctx = lambda cols, off: pl.BlockSpec((tc, cols), lambda b, h, off=off: (b, off + h))
    const = lambda shape: pl.BlockSpec(shape, lambda b, h: (0,) * len(shape))
    return [lat(DK, 0), lat(DK, HEADS), lat(DV, HEADS), lat(DV, 2 * HEADS),
            pl.BlockSpec((N_DIR, t, DK), lambda b, h: (0, b, h)),
            ctx(DK, 0), ctx(DK, HEADS), ctx(DV, HEADS), ctx(DV, 2 * HEADS),
            pl.BlockSpec((N_DIR, tc, DK), lambda b, h: (0, b, h)),
            const((1, DV)), const(seg.shape), const(later.shape), const(pair.shape)]


def _mlstm_chunk(d, qc, kc, vc, gates, tri, tri_t, eye, ct_ref, n_ref, m_ref, want_out):
    li_row = gates[2 * d:2 * d + 1]
    lf_row = _log_sigmoid(gates[2 * d + 1:2 * d + 2])
    li_col = jnp.sum(eye * li_row, axis=1, keepdims=True)
    lf_col = jnp.sum(eye * lf_row, axis=1, keepdims=True)
    fcum_col = jnp.sum(tri * lf_row, axis=1, keepdims=True)
    fcum_row = jnp.sum(tri_t * lf_col, axis=0, keepdims=True)
    f_end = jnp.sum(lf_row, axis=1, keepdims=True)
    yield
    m = m_ref[d][:, 0:1]
    m_new = jnp.maximum(f_end + m, jnp.max(f_end - fcum_row + li_row, axis=1, keepdims=True))
    carry = jnp.exp(f_end + m - m_new)
    w_col = jnp.exp(f_end - fcum_col + li_col - m_new)
    m_ref[d] = jnp.broadcast_to(m_new, m_ref.shape[1:])
    kf = kc.astype(F32) * (DK ** -0.5)
    kw = kf * w_col
    yield
    out = None
    if want_out:
        qf = qc.astype(F32)
        d_log = fcum_col - fcum_row + li_row
        causal = tri > 0.5
        inter_log = fcum_col + m
        m_q = jnp.maximum(inter_log, jnp.max(jnp.where(causal, d_log, -jnp.inf), axis=1, keepdims=True))
        p = jnp.where(causal, jnp.exp(d_log - m_q), 0.0)
        s = _mm(qc, kf.astype(BF16), NT) * p
        inter = jnp.exp(inter_log - m_q)
        yield
    ct = ct_ref[d]
    nvec = n_ref[d]
    if want_out:
        num = _mm(s.astype(BF16), vc) + inter * _mm(qc, ct.astype(BF16), NT)
        den = jnp.sum(s, axis=1, keepdims=True) + inter * jnp.sum(qf * nvec, axis=1, keepdims=True)
        out = num / jnp.maximum(jnp.abs(den), jnp.exp(-m_q))
    ct_ref[d] = carry * ct + _mm(vc, kw.astype(BF16), TN)
    n_ref[d] = carry * nvec + jnp.sum(kw, axis=0, keepdims=True)
    return out


N_MLSTM_IN = 14


def _mlstm_stream(in_refs, y_ref, cy_ref, ct_ref, n_ref, m_ref):
    (q_ref, k_ref, v_ref, g_ref, gr_ref, cq_ref, ck_ref, cv_ref, cg_ref, cgr_ref,
     bias_ref, gn_ref, tri_ref, eye_ref) = in_refs

    def finish(tot, gate):
        return (_sigmoid(gate.astype(F32)) * _rms(tot, gn_ref[...])).astype(BF16)

    def chunk(d, is_ctx, n, rows, want_out):
        qr, kr, vr, grr = (cq_ref, ck_ref, cv_ref, cgr_ref) if is_ctx else (q_ref, k_ref, v_ref, gr_ref)
        return (yield from _mlstm_chunk(d, qr[rows, :], kr[rows, :], vr[rows, :], grr[n] + bias_ref[...],
                                        tri_ref[d], tri_ref[1 - d], eye_ref[...], ct_ref, n_ref, m_ref, want_out))

    ct_ref[...] = jnp.zeros_like(ct_ref)
    n_ref[...] = jnp.zeros_like(n_ref)
    m_ref[...] = jnp.zeros_like(m_ref)
    return chunk, finish, (g_ref, y_ref), (cg_ref, cy_ref)


def _gate_rows(zs):
    n = zs.shape[0]
    g = zs[:, GLA_LR:].reshape(n // CHUNK, CHUNK, N_DIR * 2, HEADS)
    return jnp.transpose(g, (3, 0, 2, 1))


def _mlstm_specs(t, tc, consts):
    tri, eye = consts
    base = (2 * HEADS * DK + 2 * HEADS * DV)
    bq, bv = base // DK, (base + 2 * HEADS * DK) // DV
    lat = lambda cols, off: pl.BlockSpec((t, cols), lambda b, h, off=off: (b, off + h))
    ctx = lambda cols, off: pl.BlockSpec((tc, cols), lambda b, h, off=off: (b, off + h))
    const = lambda shape: pl.BlockSpec(shape, lambda b, h: (0,) * len(shape))
    return [lat(DK, bq), lat(DK, bq + HEADS), lat(DV, bv), lat(DV, bv + HEADS),
            pl.BlockSpec((None, t // CHUNK, 2 * N_DIR, CHUNK), lambda b, h: (h, b, 0, 0)),
            ctx(DK, bq), ctx(DK, bq + HEADS), ctx(DV, bv), ctx(DV, bv + HEADS),
            pl.BlockSpec((None, tc // CHUNK, 2 * N_DIR, CHUNK), lambda b, h: (h, b, 0, 0)),
            pl.BlockSpec((None, 2 * N_DIR, 1), lambda b, h: (h, 0, 0)),
            const((1, DV)), const(tri.shape), const(eye.shape)]


def _scan_kernel(*refs, with_ctx_out):
    gla_in = refs[:N_GLA_IN]
    ml_in = refs[N_GLA_IN:N_GLA_IN + N_MLSTM_IN]
    rest = refs[N_GLA_IN + N_MLSTM_IN:]
    if with_ctx_out:
        yg_ref, ym_ref, cyg_ref, cym_ref, st_ref, ct_ref, n_ref, m_ref = rest
    else:
        yg_ref, ym_ref, st_ref, ct_ref, n_ref, m_ref = rest
        cyg_ref = cym_ref = None
    streams = [_gla_stream(gla_in, yg_ref, cyg_ref, st_ref),
               _mlstm_stream(ml_in, ym_ref, cym_ref, ct_ref, n_ref, m_ref)]
    _drive_scan(gla_in[5].shape[0] // CHUNK, gla_in[0].shape[0] // CHUNK, with_ctx_out, streams)


def _scans(z, zs, la, cz, czs, cla, gla_g_norm, b_gate, mlstm_g_norm, gla_consts, mlstm_consts, batch,
           with_ctx_out):
    t = z.shape[0] // batch
    tc = cz.shape[0] // batch
    in_specs = _gla_specs(t, tc, gla_consts) + _mlstm_specs(t, tc, mlstm_consts)
    head_out = lambda rows: pl.BlockSpec((rows, DV), lambda b, h: (b, h))
    out_specs = [head_out(t), head_out(t)]
    out_shape = [jax.ShapeDtypeStruct((batch * t, HEADS * DV), BF16)] * 2
    scratch = [pltpu.VMEM((N_DIR, DV, DK), F32), pltpu.VMEM((N_DIR, DV, DK), F32),
               pltpu.VMEM((N_DIR, 1, DK), F32), pltpu.VMEM((N_DIR, 1, DK), F32)]
    if with_ctx_out:
        out_specs += [head_out(tc), head_out(tc)]
        out_shape += [jax.ShapeDtypeStruct((batch * tc, HEADS * DV), BF16)] * 2
    bias = jnp.transpose(b_gate.reshape(2 * N_DIR, HEADS))[:, :, None]
    return pl.pallas_call(
        functools.partial(_scan_kernel, with_ctx_out=with_ctx_out),
        grid=(batch, HEADS),
        in_specs=in_specs,
        out_specs=out_specs,
        out_shape=out_shape,
        scratch_shapes=scratch,
        compiler_params=_params("arbitrary", "arbitrary"),
        name="scans",
    )(z, z, z, z, la, cz, cz, cz, cz, cla, gla_g_norm, *gla_consts,
      z, z, z, z, _gate_rows(zs), cz, cz, cz, cz, _gate_rows(czs), bias, mlstm_g_norm, *mlstm_consts)


def _prep_layer_weights(w_in, w_out, w_up, conv_w, conv_b, w_down):
    hk, hv = HEADS * DK, HEADS * DV
    glr0 = 2 * hk + 2 * hv
    ml0 = glr0 + GLA_LR
    gate0 = ml0 + 2 * hk + 2 * hv
    w_main = jnp.concatenate([w_in[:, :glr0], w_in[:, ml0:gate0]], axis=1).astype(BF16)
    w_small = jnp.concatenate([w_in[:, glr0:ml0], w_in[:, gate0:]], axis=1).astype(BF16)
    pad = D_FF_PAD - D_FF
    zc = jnp.zeros((D_MODEL, pad), w_up.dtype)
    w_up_p = jnp.concatenate([w_up[:, :D_FF], zc, w_up[:, D_FF:], zc], axis=1).astype(BF16)
    conv_w9 = jnp.pad(conv_w.reshape(9, D_FF), ((0, 0), (0, pad)))
    conv_bp = jnp.pad(conv_b.reshape(1, D_FF), ((0, 0), (0, pad)))
    w_down_p = jnp.pad(w_down, ((0, pad), (0, 0))).astype(BF16)
    return w_main, w_small, w_out.astype(BF16), w_up_p, conv_w9, conv_bp, w_down_p


def kernel(x, c, ctx, c_ctx, w_mod, b_mod, g_norm1, g_norm2, w_in, gla_w_lr, gla_b_lr, mlstm_b_gate,
           gla_g_norm, mlstm_g_norm, w_out, w_up, conv_w, conv_b, w_down, g_final):
    batch, seq, _ = x.shape
    ctx_len = ctx.shape[1]
    assert batch < MOD_ROWS and seq == GRID_W * GRID_W and ctx_len % CHUNK == 0
    tm_lat = 512
    tm_ctx = min(512, batch * ctx_len)
    cc = jnp.zeros((MOD_ROWS, D_MODEL), F32).at[:batch].set(c).at[batch].set(c_ctx)
    mod = _modulation(cc, w_mod, b_mod)
    gla_consts = _gla_constants()
    mlstm_consts = _mlstm_constants()

    xl = x.reshape(batch * seq, D_MODEL)
    xc = ctx.reshape(batch * ctx_len, D_MODEL)
    row = lambda v: v.reshape(1, -1)
    hl = hc = out = None
    for l in range(DEPTH):
        last = l == DEPTH - 1
        w_main, w_small, w_out_b, w_up_p, conv_w9, conv_bp, w_down_p = _prep_layer_weights(
            w_in[l], w_out[l], w_up[l], conv_w[l], conv_b[l], w_down[l])
        if l == 0:
            z, zs = _norm_inproj(xl, row(g_norm1[0]), mod, 0, seq, None, w_main, w_small, 1024, 1024)
            cz, czs = _norm_inproj(xc, row(g_norm1[0]), mod, 0, tm_ctx, batch, w_main, w_small, tm_ctx, 1024)
        else:
            z, zs = _inproj(hl, w_main, w_small, 1024, 1024)
            cz, czs = _inproj(hc, w_main, w_small, tm_ctx, 1024)
        b_lr = gla_b_lr[l].reshape(N_DIR, 1, HEADS * DK)
        la = _decay(zs, gla_w_lr[l], b_lr, 1024)
        cla = _decay(czs, gla_w_lr[l], b_lr, tm_ctx)
        ys = _scans(z, zs, la, cz, czs, cla, row(gla_g_norm[l]), mlstm_b_gate[l],
                    row(mlstm_g_norm[l]), gla_consts, mlstm_consts, batch, not last)
        x1, h2 = _outproj(ys[0], ys[1], w_out_b, xl, row(g_norm2[l]), mod, l, seq, None, 256)
        act = _upconv(h2, w_up_p, conv_w9, conv_bp, seq, GRID_W, True, 256)
        if last:
            out = _down(act, w_down_p, x1, mod, l, row(g_final), seq, None, True, tm_lat)
        else:
            xl, hl = _down(act, w_down_p, x1, mod, l, row(g_norm1[l + 1]), seq, None, False, tm_lat)
            c1, hc2 = _outproj(ys[2], ys[3], w_out_b, xc, row(g_norm2[l]), mod, l, 256, batch, 256)
            cact = _upconv(hc2, w_up_p, conv_w9, conv_bp, batch * ctx_len, ctx_len, False, ctx_len)
            xc, hc = _down(cact, w_down_p, c1, mod, l, row(g_norm1[l + 1]), tm_ctx, batch, False, tm_ctx)
    return out.reshape(batch, seq, D_MODEL)
```

```python
import functools

import numpy as np
import jax
import jax.numpy as jnp
from jax import lax
from jax.experimental import pallas as pl
from jax.experimental.pallas import tpu as pltpu

D_MODEL = 2048
DEPTH = 2
GRID_W = 64
N_DIR = 2
HEADS = 4
DK = 128
DV = 256
GLA_LR = 16
GLA_NORMALIZER = 16.0
CHUNK = 64
D_FF = 5504
D_FF_PAD = 5632
FF_TILE = 512
UPCONV_TILE = 512
EPS = 1e-6
LANE = 128
SCAN_UNROLL = 4
N_LEVELS = 6
N_MOD = 6
MOD_ROWS = 8

VMEM_LIMIT = 56 * 1024 * 1024

F32 = jnp.float32
BF16 = jnp.bfloat16
NN = (((1,), (0,)), ((), ()))
NT = (((1,), (1,)), ((), ()))
TN = (((0,), (0,)), ((), ()))


def _mm(a, b, dims=NN):
    return lax.dot_general(a, b, dims, preferred_element_type=F32)


def _split(x):
    hi = x.astype(BF16)
    return hi, (x - hi.astype(F32)).astype(BF16)


def _mm3(a, b):
    ah, al = _split(a)
    bh, bl = _split(b)
    return _mm(ah, bh) + (_mm(ah, bl) + _mm(al, bh))


def _sigmoid(x):
    return 1.0 / (1.0 + jnp.exp(-x))


def _log_sigmoid(x):
    return jnp.minimum(x, 0.0) - jnp.log(1.0 + jnp.exp(-jnp.abs(x)))


def _rms(x, g):
    return x * lax.rsqrt(jnp.mean(x * x, axis=-1, keepdims=True) + EPS) * g


def _params(*sem):
    return pltpu.CompilerParams(dimension_semantics=sem, vmem_limit_bytes=VMEM_LIMIT)


def _mod_kernel(cc_ref, w_ref, b_ref, o_ref):
    a = cc_ref[...]
    a = a * _sigmoid(a)
    o_ref[...] = _mm3(a, w_ref[...]) + b_ref[...]


def _modulation(cc, w_mod, b_mod):
    tn = 1024
    out = pl.pallas_call(
        _mod_kernel,
        grid=(DEPTH, N_MOD * D_MODEL // tn),
        in_specs=[pl.BlockSpec((MOD_ROWS, D_MODEL), lambda l, j: (0, 0)),
                  pl.BlockSpec((None, D_MODEL, tn), lambda l, j: (l, 0, j)),
                  pl.BlockSpec((None, 1, tn), lambda l, j: (l, 0, j))],
        out_specs=pl.BlockSpec((None, MOD_ROWS, tn), lambda l, j: (l, 0, j)),
        out_shape=jax.ShapeDtypeStruct((DEPTH, MOD_ROWS, N_MOD * D_MODEL), F32),
        compiler_params=_params("arbitrary", "arbitrary"),
        name="modulation",
    )(cc, w_mod, b_mod.reshape(DEPTH, 1, N_MOD * D_MODEL))
    return out.reshape(DEPTH * MOD_ROWS * N_MOD, 1, D_MODEL)


def _mod_spec(layer, piece, seq, ctx_row, tm):
    base = layer * MOD_ROWS * N_MOD + piece
    assert seq % tm == 0
    blocks_per_batch = seq // tm

    def idx(i, *_):
        r = ctx_row if ctx_row is not None else i // blocks_per_batch
        return (base + r * N_MOD, 0, 0)

    return pl.BlockSpec((None, 1, D_MODEL), idx)


def _inproj_kernel(h_ref, w_ref, ws_ref, z_ref, zs_ref):
    h = h_ref[...]
    z_ref[...] = _mm(h, w_ref[...]).astype(z_ref.dtype)

    @pl.when(pl.program_id(1) == 0)
    def _():
        zs_ref[...] = _mm(h, ws_ref[...])


def _inproj(h, w_main, w_small, tm, tn):
    n = h.shape[0]
    p = w_main.shape[1]
    ps = w_small.shape[1]
    return pl.pallas_call(
        _inproj_kernel,
        grid=(n // tm, p // tn),
        in_specs=[pl.BlockSpec((tm, D_MODEL), lambda i, j: (i, 0)),
                  pl.BlockSpec((D_MODEL, tn), lambda i, j: (0, j)),
                  pl.BlockSpec((D_MODEL, ps), lambda i, j: (0, 0))],
        out_specs=[pl.BlockSpec((tm, tn), lambda i, j: (i, j)),
                   pl.BlockSpec((tm, ps), lambda i, j: (i, 0))],
        out_shape=[jax.ShapeDtypeStruct((n, p), BF16), jax.ShapeDtypeStruct((n, ps), F32)],
        compiler_params=_params("arbitrary", "arbitrary"),
        name="inproj",
    )(h, w_main, w_small)


def _norm_inproj_kernel(x_ref, g_ref, sc_ref, sh_ref, w_ref, ws_ref, z_ref, zs_ref, h_ref):
    @pl.when(pl.program_id(1) == 0)
    def _():
        h = (_rms(x_ref[...], g_ref[...]) * (1.0 + sc_ref[...]) + sh_ref[...]).astype(h_ref.dtype)
        h_ref[...] = h
        zs_ref[...] = _mm(h, ws_ref[...])

    z_ref[...] = _mm(h_ref[...], w_ref[...]).astype(z_ref.dtype)


def _norm_inproj(x, g, mod, layer, seq, ctx_row, w_main, w_small, tm, tn):
    n = x.shape[0]
    p = w_main.shape[1]
    ps = w_small.shape[1]
    return pl.pallas_call(
        _norm_inproj_kernel,
        grid=(n // tm, p // tn),
        in_specs=[pl.BlockSpec((tm, D_MODEL), lambda i, j: (i, 0)),
                  pl.BlockSpec((1, D_MODEL), lambda i, j: (0, 0)),
                  _mod_spec(layer, 1, seq, ctx_row, tm),
                  _mod_spec(layer, 0, seq, ctx_row, tm),
                  pl.BlockSpec((D_MODEL, tn), lambda i, j: (0, j)),
                  pl.BlockSpec((D_MODEL, ps), lambda i, j: (0, 0))],
        out_specs=[pl.BlockSpec((tm, tn), lambda i, j: (i, j)),
                   pl.BlockSpec((tm, ps), lambda i, j: (i, 0))],
        out_shape=[jax.ShapeDtypeStruct((n, p), BF16), jax.ShapeDtypeStruct((n, ps), F32)],
        scratch_shapes=[pltpu.VMEM((tm, D_MODEL), BF16)],
        compiler_params=_params("arbitrary", "arbitrary"),
        name="norm_inproj",
    )(x, g, mod, mod, w_main, w_small)


def _outproj_kernel(yg_ref, ym_ref, wg_ref, wm_ref, x_ref, gt_ref, g2_ref, sc_ref, sh_ref, x1_ref, h2_ref):
    acc = _mm(yg_ref[...], wg_ref[...]) + _mm(ym_ref[...], wm_ref[...])
    x1 = x_ref[...] + gt_ref[...] * acc
    x1_ref[...] = x1
    h2_ref[...] = (_rms(x1, g2_ref[...]) * (1.0 + sc_ref[...]) + sh_ref[...]).astype(h2_ref.dtype)


def _outproj(yg, ym, w_out, x, g2, mod, layer, seq, ctx_row, tm):
    n = x.shape[0]
    half = w_out.shape[0] // 2
    row = lambda i: (i, 0)
    return pl.pallas_call(
        _outproj_kernel,
        grid=(n // tm,),
        in_specs=[pl.BlockSpec((tm, half), row),
                  pl.BlockSpec((tm, half), row),
                  pl.BlockSpec((half, D_MODEL), lambda i: (0, 0)),
                  pl.BlockSpec((half, D_MODEL), lambda i: (1, 0)),
                  pl.BlockSpec((tm, D_MODEL), row),
                  _mod_spec(layer, 2, seq, ctx_row, tm),
                  pl.BlockSpec((1, D_MODEL), lambda i: (0, 0)),
                  _mod_spec(layer, 4, seq, ctx_row, tm),
                  _mod_spec(layer, 3, seq, ctx_row, tm)],
        out_specs=[pl.BlockSpec((tm, D_MODEL), row), pl.BlockSpec((tm, D_MODEL), row)],
        out_shape=[jax.ShapeDtypeStruct((n, D_MODEL), F32), jax.ShapeDtypeStruct((n, D_MODEL), BF16)],
        compiler_params=_params("arbitrary"),
        name="outproj",
    )(yg, ym, w_out, w_out, x, mod, g2, mod, mod)


def _upconv_kernel(h_ref, wg_ref, wv_ref, cw_ref, cb_ref, act_ref, ug_ref, uv_ref, *, grid_w, vertical, chunk_rows):
    n_tok, tf = ug_ref.shape
    n_rows = n_tok // grid_w
    rows_per_chunk = chunk_rows // grid_w
    n_chunks = n_tok // chunk_rows
    col = lax.broadcasted_iota(jnp.int32, (grid_w, LANE), 0)

    def project(k):
        rows = slice(k * chunk_rows, (k + 1) * chunk_rows)
        ug_ref[rows, :] = _mm(h_ref[rows, :], wg_ref[...]).astype(ug_ref.dtype)
        uv_ref[rows, :] = _mm(h_ref[rows, :], wv_ref[...]).astype(uv_ref.dtype)

    def conv(k):
        for c in range(tf // LANE):
            lanes = slice(c * LANE, (c + 1) * LANE)
            cw = cw_ref[:, lanes]
            cb = cb_ref[:, lanes]
            for r in range(k * rows_per_chunk, (k + 1) * rows_per_chunk):
                rows = slice(r * grid_w, (r + 1) * grid_w)
                mid = ug_ref[rows, lanes].astype(F32)
                taps = [cw[3 + j:4 + j] * mid for j in range(3)]
                if vertical and r > 0:
                    up = ug_ref[(r - 1) * grid_w:r * grid_w, lanes].astype(F32)
                    taps = [t + cw[j:j + 1] * up for j, t in enumerate(taps)]
                if vertical and r < n_rows - 1:
                    down = ug_ref[(r + 1) * grid_w:(r + 2) * grid_w, lanes].astype(F32)
                    taps = [t + cw[6 + j:7 + j] * down for j, t in enumerate(taps)]
                left = jnp.where(col != 0, pltpu.roll(taps[0], 1, 0), 0.0)
                right = jnp.where(col != grid_w - 1, pltpu.roll(taps[2], grid_w - 1, 0), 0.0)
                g = taps[1] + left + right + cb
                half_g = 0.5 * g
                silu = half_g * (1.0 + jnp.tanh(half_g))
                act_ref[rows, lanes] = (silu * uv_ref[rows, lanes].astype(F32)).astype(act_ref.dtype)

    project(0)
    for k in range(1, n_chunks):
        project(k)
        conv(k - 1)
    conv(n_chunks - 1)


def _upconv(h, w_up, conv_w9, conv_b, img_tokens, grid_w, vertical, chunk_rows):
    n = h.shape[0]
    tf = UPCONV_TILE
    nf = D_FF_PAD // tf
    return pl.pallas_call(
        functools.partial(_upconv_kernel, grid_w=grid_w, vertical=vertical, chunk_rows=chunk_rows),
        grid=(n // img_tokens, nf),
        in_specs=[pl.BlockSpec((img_tokens, D_MODEL), lambda b, f: (b, 0), pipeline_mode=pl.Buffered(1)),
                  pl.BlockSpec((D_MODEL, tf), lambda b, f: (0, f)),
                  pl.BlockSpec((D_MODEL, tf), lambda b, f: (0, nf + f)),
                  pl.BlockSpec((9, tf), lambda b, f: (0, f)),
                  pl.BlockSpec((1, tf), lambda b, f: (0, f))],
        out_specs=pl.BlockSpec((img_tokens, tf), lambda b, f: (b, f)),
        out_shape=jax.ShapeDtypeStruct((n, D_FF_PAD), BF16),
        scratch_shapes=[pltpu.VMEM((img_tokens, tf), BF16), pltpu.VMEM((img_tokens, tf), BF16)],
        compiler_params=_params("arbitrary", "arbitrary"),
        name="upconv",
    )(h, w_up, w_up, conv_w9, conv_b)


def _down_kernel(*refs, final):
    act_ref, wd_ref, x1_ref, gt_ref, gn_ref = refs[:5]
    if final:
        out_ref, acc_ref = refs[5:]
    else:
        sc_ref, sh_ref, x2_ref, hn_ref, acc_ref = refs[5:]
    f = pl.program_id(1)

    @pl.when(f == 0)
    def _():
        acc_ref[...] = jnp.zeros_like(acc_ref)

    acc_ref[...] += _mm(act_ref[...], wd_ref[...])

    @pl.when(f == pl.num_programs(1) - 1)
    def _():
        x2 = x1_ref[...] + gt_ref[...] * acc_ref[...]
        if final:
            out_ref[...] = _rms(x2, gn_ref[...])
        else:
            x2_ref[...] = x2
            hn_ref[...] = (_rms(x2, gn_ref[...]) * (1.0 + sc_ref[...]) + sh_ref[...]).astype(hn_ref.dtype)


def _down(act, w_down, x1, mod, layer, gn, seq, ctx_row, final, tm):
    n = x1.shape[0]
    tk = FF_TILE
    row = lambda i, f: (i, 0)
    in_specs = [pl.BlockSpec((tm, tk), lambda i, f: (i, f)),
                pl.BlockSpec((tk, D_MODEL), lambda i, f: (f, 0)),
                pl.BlockSpec((tm, D_MODEL), row),
                _mod_spec(layer, 5, seq, ctx_row, tm),
                pl.BlockSpec((1, D_MODEL), lambda i, f: (0, 0))]
    args = [act, w_down, x1, mod, gn]
    if final:
        out_specs = pl.BlockSpec((tm, D_MODEL), row)
        out_shape = jax.ShapeDtypeStruct((n, D_MODEL), F32)
    else:
        in_specs += [_mod_spec(layer + 1, 1, seq, ctx_row, tm),
                     _mod_spec(layer + 1, 0, seq, ctx_row, tm)]
        args += [mod, mod]
        out_specs = [pl.BlockSpec((tm, D_MODEL), row), pl.BlockSpec((tm, D_MODEL), row)]
        out_shape = [jax.ShapeDtypeStruct((n, D_MODEL), F32), jax.ShapeDtypeStruct((n, D_MODEL), BF16)]
    return pl.pallas_call(
        functools.partial(_down_kernel, final=final),
        grid=(n // tm, D_FF_PAD // tk),
        in_specs=in_specs,
        out_specs=out_specs,
        out_shape=out_shape,
        scratch_shapes=[pltpu.VMEM((tm, D_MODEL), F32)],
        compiler_params=_params("arbitrary", "arbitrary"),
        name="down",
    )(*args)


def _decay_kernel(zs_ref, wlr_ref, blr_ref, la_ref):
    glr = zs_ref[:, 0:GLA_LR]
    for d in range(N_DIR):
        la_ref[d] = _log_sigmoid(_mm3(glr, wlr_ref[d]) + blr_ref[d]) * (1.0 / GLA_NORMALIZER)


def _decay(zs, w_lr, b_lr, tm):
    n = zs.shape[0]
    width = w_lr.shape[-1]
    return pl.pallas_call(
        _decay_kernel,
        grid=(n // tm,),
        in_specs=[pl.BlockSpec((tm, zs.shape[1]), lambda i: (i, 0)),
                  pl.BlockSpec(w_lr.shape, lambda i: (0, 0, 0)),
                  pl.BlockSpec(b_lr.shape, lambda i: (0, 0, 0))],
        out_specs=pl.BlockSpec((N_DIR, tm, width), lambda i: (0, i, 0)),
        out_shape=jax.ShapeDtypeStruct((N_DIR, n, width), F32),
        compiler_params=_params("arbitrary"),
        name="decay",
    )(zs, w_lr, b_lr)


def _flip2(a):
    return a[..., ::-1, ::-1]


def _gla_constants():
    c = CHUNK
    m = np.zeros((N_LEVELS + 2, c, c), np.float32)
    later = np.zeros((N_LEVELS, c, 1), np.float32)
    pair = np.zeros((N_LEVELS + 1, c, c), np.float32)
    for lvl in range(N_LEVELS):
        n = c >> lvl
        half = n // 2
        for r in range(c):
            bnd = (r // n) * n + half - 1
            if r % n >= half:
                m[lvl, r, bnd + 1:r + 1] = 1.0
                later[lvl, r] = 1.0
            else:
                m[lvl, r, r + 1:bnd + 1] = 1.0
        blk = np.arange(c) // n
        pos = np.arange(c) % n
        pair[lvl] = (blk[:, None] == blk[None, :]) & (pos[:, None] >= half) & (pos[None, :] < half)
    pair[N_LEVELS] = np.eye(c)
    for r in range(c):
        m[N_LEVELS, r, :r + 1] = 1.0
        m[N_LEVELS + 1, r, r + 1:] = 1.0
    m = np.stack([m, _flip2(m)]).reshape(N_DIR, (N_LEVELS + 2) * c, c)
    later = np.stack([later, later[:, ::-1]])
    later = np.broadcast_to(later, (N_DIR, N_LEVELS, c, DK))
    pair = np.stack([pair, _flip2(pair)])
    return jnp.asarray(m, BF16), jnp.asarray(later, F32), jnp.asarray(pair, F32)


def _mlstm_constants():
    tri = np.tril(np.ones((CHUNK, CHUNK), np.float32))
    return jnp.asarray(np.stack([tri, tri.T]), F32), jnp.asarray(np.eye(CHUNK), F32)


def _drive_scan(n_ctx, n_lat, with_ctx_out, streams):
    def one(stream, d, is_ctx, s, n_total, second_half, want_out):
        chunk, finish, lat_io, ctx_io = stream
        gate_ref, y_ref = ctx_io if is_ctx else lat_io
        n = s if d == 0 else n_total - 1 - s
        start = n * CHUNK if isinstance(n, int) else pl.multiple_of(n * CHUNK, CHUNK)
        rows = pl.ds(start, CHUNK)
        o = yield from chunk(d, is_ctx, n, rows, want_out)
        if not want_out:
            return
        if second_half:
            y_ref[rows, :] = finish(y_ref[rows, :].astype(F32) + o, gate_ref[rows, :])
        else:
            y_ref[rows, :] = o.astype(y_ref.dtype)

    def steps(is_ctx, ss, n_total, second_half, want_out):
        waiting = [[one(stream, d, is_ctx, s, n_total, second_half, want_out)
                    for d in range(N_DIR) for stream in streams] for s in ss]
        live = []
        while live or waiting:
            if waiting:
                live.extend(waiting.pop(0))
            for g in list(live):
                try:
                    next(g)
                except StopIteration:
                    live.remove(g)

    assert n_ctx % 2 == 0 and n_lat % (2 * SCAN_UNROLL) == 0
    for s in range(0, n_ctx, n_ctx // 2):
        steps(True, [s + u for u in range(n_ctx // 2)], n_ctx, s >= n_ctx // 2, with_ctx_out)
    per_half = n_lat // (2 * SCAN_UNROLL)
    for half in range(2):
        def body(k, carry, half=half):
            steps(False, [k * SCAN_UNROLL + u for u in range(SCAN_UNROLL)], n_lat, half == 1, True)
            return carry

        lax.fori_loop(half * per_half, (half + 1) * per_half, body, 0)


def _gla_chunk(d, qc, kc, vc, la, seg_ref, later_ref, pair_ref, st_ref, want_out):
    c = CHUNK
    la_hi, la_lo = _split(la)
    seg = seg_ref[d]
    e_all = jnp.exp(_mm(seg, la_hi) + _mm(seg, la_lo))
    yield
    e_cum = e_all[N_LEVELS * c:(N_LEVELS + 1) * c]
    e_end = e_all[(N_LEVELS + 1) * c:(N_LEVELS + 2) * c]
    last = c - 1 if d == 0 else 0
    decay = e_cum[last:last + 1]
    qf = qc.astype(F32) * (DK ** -0.5)
    kf = kc.astype(F32)
    out = None
    if want_out:
        a = pair_ref[d, N_LEVELS] * _mm(qf.astype(BF16), kc, NT)
        for lvl in range(N_LEVELS):
            e = e_all[lvl * c:(lvl + 1) * c]
            later = later_ref[d, lvl]
            ql = (qf * e * later).astype(BF16)
            kl = (kf * e * (1.0 - later)).astype(BF16)
            a = a + pair_ref[d, lvl] * _mm(ql, kl, NT)
            yield
    st = st_ref[d]
    if want_out:
        out = _mm(a.astype(BF16), vc) + _mm((qf * e_cum).astype(BF16), st.astype(BF16), NT)
    st_ref[d] = st * decay + _mm(vc, (kf * e_end).astype(BF16), TN)
    return out


N_GLA_IN = 14


def _gla_stream(in_refs, y_ref, cy_ref, st_ref):
    (q_ref, k_ref, v_ref, g_ref, la_ref, cq_ref, ck_ref, cv_ref, cg_ref, cla_ref,
     gn_ref, seg_ref, later_ref, pair_ref) = in_refs

    def finish(tot, gate):
        gate = gate.astype(F32)
        return (_rms(tot, gn_ref[...]) * (gate * _sigmoid(gate))).astype(BF16)

    def chunk(d, is_ctx, n, rows, want_out):
        del n
        qr, kr, vr, lr = (cq_ref, ck_ref, cv_ref, cla_ref) if is_ctx else (q_ref, k_ref, v_ref, la_ref)
        return (yield from _gla_chunk(d, qr[rows, :], kr[rows, :], vr[rows, :], lr[d, rows, :],
                                      seg_ref, later_ref, pair_ref, st_ref, want_out))

    st_ref[...] = jnp.zeros_like(st_ref)
    return chunk, finish, (g_ref, y_ref), (cg_ref, cy_ref)


def _gla_specs(t, tc, consts):
    seg, later, pair = consts
    lat = lambda cols, off: pl.BlockSpec((t, cols), lambda b, h, off=off: (b, off + h))
    ctx = lambda cols, off: pl.BlockSpec((tc, cols), lambda b, h, off=off: (b, off + h))
    const = lambda shape: pl.BlockSpec(shape, lambda b, h: (0,) * len(shape))
    return [lat(DK, 0), lat(DK, HEADS), lat(DV, HEADS), lat(DV, 2 * HEADS),
            pl.BlockSpec((N_DIR, t, DK), lambda b, h: (0, b, h)),
            ctx(DK, 0), ctx(DK, HEADS), ctx(DV, HEADS), ctx(DV, 2 * HEADS),
            pl.BlockSpec((N_DIR, tc, DK), lambda b, h: (0, b, h)),
            const((1, DV)), const(seg.shape), const(later.shape), const(pair.shape)]


def _mlstm_chunk(d, qc, kc, vc, gates, tri, tri_t, eye, ct_ref, n_ref, m_ref, want_out):
    li_row = gates[2 * d:2 * d + 1]
    lf_row = _log_sigmoid(gates[2 * d + 1:2 * d + 2])
    li_col = jnp.sum(eye * li_row, axis=1, keepdims=True)
    lf_col = jnp.sum(eye * lf_row, axis=1, keepdims=True)
    fcum_col = jnp.sum(tri * lf_row, axis=1, keepdims=True)
    fcum_row = jnp.sum(tri_t * lf_col, axis=0, keepdims=True)
    f_end = jnp.sum(lf_row, axis=1, keepdims=True)
    yield
    m = m_ref[d][:, 0:1]
    m_new = jnp.maximum(f_end + m, jnp.max(f_end - fcum_row + li_row, axis=1, keepdims=True))
    carry = jnp.exp(f_end + m - m_new)
    w_col = jnp.exp(f_end - fcum_col + li_col - m_new)
    m_ref[d] = jnp.broadcast_to(m_new, m_ref.shape[1:])
    kf = kc.astype(F32) * (DK ** -0.5)
    kw = kf * w_col
    yield
    out = None
    if want_out:
        qf = qc.astype(F32)
        d_log = fcum_col - fcum_row + li_row
        causal = tri > 0.5
        inter_log = fcum_col + m
        m_q = jnp.maximum(inter_log, jnp.max(jnp.where(causal, d_log, -jnp.inf), axis=1, keepdims=True))
        p = jnp.where(causal, jnp.exp(d_log - m_q), 0.0)
        s = _mm(qc, kf.astype(BF16), NT) * p
        inter = jnp.exp(inter_log - m_q)
        yield
    ct = ct_ref[d]
    nvec = n_ref[d]
    if want_out:
        num = _mm(s.astype(BF16), vc) + inter * _mm(qc, ct.astype(BF16), NT)
        den = jnp.sum(s, axis=1, keepdims=True) + inter * jnp.sum(qf * nvec, axis=1, keepdims=True)
        out = num / jnp.maximum(jnp.abs(den), jnp.exp(-m_q))
    ct_ref[d] = carry * ct + _mm(vc, kw.astype(BF16), TN)
    n_ref[d] = carry * nvec + jnp.sum(kw, axis=0, keepdims=True)
    return out


N_MLSTM_IN = 14


def _mlstm_stream(in_refs, y_ref, cy_ref, ct_ref, n_ref, m_ref):
    (q_ref, k_ref, v_ref, g_ref, gr_ref, cq_ref, ck_ref, cv_ref, cg_ref, cgr_ref,
     bias_ref, gn_ref, tri_ref, eye_ref) = in_refs

    def finish(tot, gate):
        return (_sigmoid(gate.astype(F32)) * _rms(tot, gn_ref[...])).astype(BF16)

    def chunk(d, is_ctx, n, rows, want_out):
        qr, kr, vr, grr = (cq_ref, ck_ref, cv_ref, cgr_ref) if is_ctx else (q_ref, k_ref, v_ref, gr_ref)
        return (yield from _mlstm_chunk(d, qr[rows, :], kr[rows, :], vr[rows, :], grr[n] + bias_ref[...],
                                        tri_ref[d], tri_ref[1 - d], eye_ref[...], ct_ref, n_ref, m_ref, want_out))

    ct_ref[...] = jnp.zeros_like(ct_ref)
    n_ref[...] = jnp.zeros_like(n_ref)
    m_ref[...] = jnp.zeros_like(m_ref)
    return chunk, finish, (g_ref, y_ref), (cg_ref, cy_ref)


def _gate_rows(zs):
    n = zs.shape[0]
    g = zs[:, GLA_LR:].reshape(n // CHUNK, CHUNK, N_DIR * 2, HEADS)
    return jnp.transpose(g, (3, 0, 2, 1))


def _mlstm_specs(t, tc, consts):
    tri, eye = consts
    base = (2 * HEADS * DK + 2 * HEADS * DV)
    bq, bv = base // DK, (base + 2 * HEADS * DK) // DV
    lat = lambda cols, off: pl.BlockSpec((t, cols), lambda b, h, off=off: (b, off + h))
    ctx = lambda cols, off: pl.BlockSpec((tc, cols), lambda b, h, off=off: (b, off + h))
    const = lambda shape: pl.BlockSpec(shape, lambda b, h: (0,) * len(shape))
    return [lat(DK, bq), lat(DK, bq + HEADS), lat(DV, bv), lat(DV, bv + HEADS),
            pl.BlockSpec((None, t // CHUNK, 2 * N_DIR, CHUNK), lambda b, h: (h, b, 0, 0)),
            ctx(DK, bq), ctx(DK, bq + HEADS), ctx(DV, bv), ctx(DV, bv + HEADS),
            pl.BlockSpec((None, tc // CHUNK, 2 * N_DIR, CHUNK), lambda b, h: (h, b, 0, 0)),
            pl.BlockSpec((None, 2 * N_DIR, 1), lambda b, h: (h, 0, 0)),
            const((1, DV)), const(tri.shape), const(eye.shape)]


def _scan_kernel(*refs, with_ctx_out):
    gla_in = refs[:N_GLA_IN]
    ml_in = refs[N_GLA_IN:N_GLA_IN + N_MLSTM_IN]
    rest = refs[N_GLA_IN + N_MLSTM_IN:]
    if with_ctx_out:
        yg_ref, ym_ref, cyg_ref, cym_ref, st_ref, ct_ref, n_ref, m_ref = rest
    else:
        yg_ref, ym_ref, st_ref, ct_ref, n_ref, m_ref = rest
        cyg_ref = cym_ref = None
    streams = [_gla_stream(gla_in, yg_ref, cyg_ref, st_ref),
               _mlstm_stream(ml_in, ym_ref, cym_ref, ct_ref, n_ref, m_ref)]
    _drive_scan(gla_in[5].shape[0] // CHUNK, gla_in[0].shape[0] // CHUNK, with_ctx_out, streams)


def _scans(z, zs, la, cz, czs, cla, gla_g_norm, b_gate, mlstm_g_norm, gla_consts, mlstm_consts, batch,
           with_ctx_out):
    t = z.shape[0] // batch
    tc = cz.shape[0] // batch
    in_specs = _gla_specs(t, tc, gla_consts) + _mlstm_specs(t, tc, mlstm_consts)
    head_out = lambda rows: pl.BlockSpec((rows, DV), lambda b, h: (b, h))
    out_specs = [head_out(t), head_out(t)]
    out_shape = [jax.ShapeDtypeStruct((batch * t, HEADS * DV), BF16)] * 2
    scratch = [pltpu.VMEM((N_DIR, DV, DK), F32), pltpu.VMEM((N_DIR, DV, DK), F32),
               pltpu.VMEM((N_DIR, 1, DK), F32), pltpu.VMEM((N_DIR, 1, DK), F32)]
    if with_ctx_out:
        out_specs += [head_out(tc), head_out(tc)]
        out_shape += [jax.ShapeDtypeStruct((batch * tc, HEADS * DV), BF16)] * 2
    bias = jnp.transpose(b_gate.reshape(2 * N_DIR, HEADS))[:, :, None]
    return pl.pallas_call(
        functools.partial(_scan_kernel, with_ctx_out=with_ctx_out),
        grid=(batch, HEADS),
        in_specs=in_specs,
        out_specs=out_specs,
        out_shape=out_shape,
        scratch_shapes=scratch,
        compiler_params=_params("arbitrary", "arbitrary"),
        name="scans",
    )(z, z, z, z, la, cz, cz, cz, cz, cla, gla_g_norm, *gla_consts,
      z, z, z, z, _gate_rows(zs), cz, cz, cz, cz, _gate_rows(czs), bias, mlstm_g_norm, *mlstm_consts)


def _prep_layer_weights(w_in, w_out, w_up, conv_w, conv_b, w_down):
    hk, hv = HEADS * DK, HEADS * DV
    glr0 = 2 * hk + 2 * hv
    ml0 = glr0 + GLA_LR
    gate0 = ml0 + 2 * hk + 2 * hv
    w_in = w_in.astype(BF16)
    w_main = jnp.concatenate([w_in[:, :glr0], w_in[:, ml0:gate0]], axis=1)
    w_small = jnp.concatenate([w_in[:, glr0:ml0], w_in[:, gate0:]], axis=1)
    pad = D_FF_PAD - D_FF
    w_up = w_up.astype(BF16)
    zc = jnp.zeros((D_MODEL, pad), BF16)
    w_up_p = jnp.concatenate([w_up[:, :D_FF], zc, w_up[:, D_FF:], zc], axis=1)
    conv_w9 = jnp.pad(conv_w.reshape(9, D_FF), ((0, 0), (0, pad)))
    conv_bp = jnp.pad(conv_b.reshape(1, D_FF), ((0, 0), (0, pad)))
    w_down_p = jnp.pad(w_down.astype(BF16), ((0, pad), (0, 0)))
    return w_main, w_small, w_out.astype(BF16), w_up_p, conv_w9, conv_bp, w_down_p


def kernel(x, c, ctx, c_ctx, w_mod, b_mod, g_norm1, g_norm2, w_in, gla_w_lr, gla_b_lr, mlstm_b_gate,
           gla_g_norm, mlstm_g_norm, w_out, w_up, conv_w, conv_b, w_down, g_final):
    batch, seq, _ = x.shape
    ctx_len = ctx.shape[1]
    assert batch < MOD_ROWS and seq == GRID_W * GRID_W and ctx_len % CHUNK == 0
    tm_lat = 512
    tm_ctx = min(512, batch * ctx_len)
    cc = jnp.zeros((MOD_ROWS, D_MODEL), F32).at[:batch].set(c).at[batch].set(c_ctx)
    mod = _modulation(cc, w_mod, b_mod)
    gla_consts = _gla_constants()
    mlstm_consts = _mlstm_constants()

    xl = x.reshape(batch * seq, D_MODEL)
    xc = ctx.reshape(batch * ctx_len, D_MODEL)
    row = lambda v: v.reshape(1, -1)
    hl = hc = out = None
    for l in range(DEPTH):
        last = l == DEPTH - 1
        w_main, w_small, w_out_b, w_up_p, conv_w9, conv_bp, w_down_p = _prep_layer_weights(
            w_in[l], w_out[l], w_up[l], conv_w[l], conv_b[l], w_down[l])
        if l == 0:
            z, zs = _norm_inproj(xl, row(g_norm1[0]), mod, 0, seq, None, w_main, w_small, 1024, 1024)
            cz, czs = _norm_inproj(xc, row(g_norm1[0]), mod, 0, tm_ctx, batch, w_main, w_small, tm_ctx, 1024)
        else:
            z, zs = _inproj(hl, w_main, w_small, 1024, 1024)
            cz, czs = _inproj(hc, w_main, w_small, tm_ctx, 1024)
        b_lr = gla_b_lr[l].reshape(N_DIR, 1, HEADS * DK)
        la = _decay(zs, gla_w_lr[l], b_lr, 1024)
        cla = _decay(czs, gla_w_lr[l], b_lr, tm_ctx)
        ys = _scans(z, zs, la, cz, czs, cla, row(gla_g_norm[l]), mlstm_b_gate[l],
                    row(mlstm_g_norm[l]), gla_consts, mlstm_consts, batch, not last)
        x1, h2 = _outproj(ys[0], ys[1], w_out_b, xl, row(g_norm2[l]), mod, l, seq, None, 512)
        act = _upconv(h2, w_up_p, conv_w9, conv_bp, seq, GRID_W, True, 256)
        if last:
            out = _down(act, w_down_p, x1, mod, l, row(g_final), seq, None, True, tm_lat)
        else:
            xl, hl = _down(act, w_down_p, x1, mod, l, row(g_norm1[l + 1]), seq, None, False, tm_lat)
            c1, hc2 = _outproj(ys[2], ys[3], w_out_b, xc, row(g_norm2[l]), mod, l, 256, batch, 256)
            cact = _upconv(hc2, w_up_p, conv_w9, conv_bp, batch * ctx_len, ctx_len, False, ctx_len)
            xc, hc = _down(cact, w_down_p, c1, mod, l, row(g_norm1[l + 1]), tm_ctx, batch, False, tm_ctx)
    return out.reshape(batch, seq, D_MODEL)
```

```python
import functools

import numpy as np
import jax
import jax.numpy as jnp
from jax import lax
from jax.experimental import pallas as pl
from jax.experimental.pallas import tpu as pltpu

D_MODEL = 2048
DEPTH = 2
GRID_W = 64
N_DIR = 2
HEADS = 4
DK = 128
DV = 256
GLA_LR = 16
GLA_NORMALIZER = 16.0
CHUNK = 64
D_FF = 5504
D_FF_PAD = 5632
FF_TILE = 512
UPCONV_TILE = 512
EPS = 1e-6
LANE = 128
SCAN_UNROLL = 4
N_LEVELS = 6
N_MOD = 6
MOD_ROWS = 8

VMEM_LIMIT = 56 * 1024 * 1024

F32 = jnp.float32
BF16 = jnp.bfloat16
NN = (((1,), (0,)), ((), ()))
NT = (((1,), (1,)), ((), ()))
TN = (((0,), (0,)), ((), ()))


def _mm(a, b, dims=NN):
    return lax.dot_general(a, b, dims, preferred_element_type=F32)


def _split(x):
    hi = x.astype(BF16)
    return hi, (x - hi.astype(F32)).astype(BF16)


def _mm3(a, b):
    ah, al = _split(a)
    bh, bl = _split(b)
    return _mm(ah, bh) + (_mm(ah, bl) + _mm(al, bh))


def _sigmoid(x):
    return 1.0 / (1.0 + jnp.exp(-x))


def _log_sigmoid(x):
    return jnp.minimum(x, 0.0) - jnp.log(1.0 + jnp.exp(-jnp.abs(x)))


def _rms(x, g):
    return x * lax.rsqrt(jnp.mean(x * x, axis=-1, keepdims=True) + EPS) * g


def _params(*sem):
    return pltpu.CompilerParams(dimension_semantics=sem, vmem_limit_bytes=VMEM_LIMIT)


def _mod_kernel(cc_ref, w_ref, b_ref, o_ref):
    a = cc_ref[...]
    a = a * _sigmoid(a)
    o_ref[...] = _mm3(a, w_ref[...]) + b_ref[...]


def _modulation(cc, w_mod, b_mod):
    tn = 1024
    out = pl.pallas_call(
        _mod_kernel,
        grid=(DEPTH, N_MOD * D_MODEL // tn),
        in_specs=[pl.BlockSpec((MOD_ROWS, D_MODEL), lambda l, j: (0, 0)),
                  pl.BlockSpec((None, D_MODEL, tn), lambda l, j: (l, 0, j)),
                  pl.BlockSpec((None, 1, tn), lambda l, j: (l, 0, j))],
        out_specs=pl.BlockSpec((None, MOD_ROWS, tn), lambda l, j: (l, 0, j)),
        out_shape=jax.ShapeDtypeStruct((DEPTH, MOD_ROWS, N_MOD * D_MODEL), F32),
        compiler_params=_params("arbitrary", "arbitrary"),
        name="modulation",
    )(cc, w_mod, b_mod.reshape(DEPTH, 1, N_MOD * D_MODEL))
    return out.reshape(DEPTH * MOD_ROWS * N_MOD, 1, D_MODEL)


def _mod_spec(layer, piece, seq, ctx_row, tm):
    base = layer * MOD_ROWS * N_MOD + piece
    assert seq % tm == 0
    blocks_per_batch = seq // tm

    def idx(i, *_):
        r = ctx_row if ctx_row is not None else i // blocks_per_batch
        return (base + r * N_MOD, 0, 0)

    return pl.BlockSpec((None, 1, D_MODEL), idx)


def _inproj_kernel(h_ref, w_ref, ws_ref, z_ref, zs_ref):
    h = h_ref[...]
    z_ref[...] = _mm(h, w_ref[...]).astype(z_ref.dtype)

    @pl.when(pl.program_id(1) == 0)
    def _():
        zs_ref[...] = _mm(h, ws_ref[...])


def _inproj(h, w_main, w_small, tm, tn):
    n = h.shape[0]
    p = w_main.shape[1]
    ps = w_small.shape[1]
    return pl.pallas_call(
        _inproj_kernel,
        grid=(n // tm, p // tn),
        in_specs=[pl.BlockSpec((tm, D_MODEL), lambda i, j: (i, 0)),
                  pl.BlockSpec((D_MODEL, tn), lambda i, j: (0, j)),
                  pl.BlockSpec((D_MODEL, ps), lambda i, j: (0, 0))],
        out_specs=[pl.BlockSpec((tm, tn), lambda i, j: (i, j)),
                   pl.BlockSpec((tm, ps), lambda i, j: (i, 0))],
        out_shape=[jax.ShapeDtypeStruct((n, p), BF16), jax.ShapeDtypeStruct((n, ps), F32)],
        compiler_params=_params("arbitrary", "arbitrary"),
        name="inproj",
    )(h, w_main, w_small)


def _norm_inproj_kernel(x_ref, g_ref, sc_ref, sh_ref, w_ref, ws_ref, z_ref, zs_ref, h_ref):
    @pl.when(pl.program_id(1) == 0)
    def _():
        h = (_rms(x_ref[...], g_ref[...]) * (1.0 + sc_ref[...]) + sh_ref[...]).astype(h_ref.dtype)
        h_ref[...] = h
        zs_ref[...] = _mm(h, ws_ref[...])

    z_ref[...] = _mm(h_ref[...], w_ref[...]).astype(z_ref.dtype)


def _norm_inproj(x, g, mod, layer, seq, ctx_row, w_main, w_small, tm, tn):
    n = x.shape[0]
    p = w_main.shape[1]
    ps = w_small.shape[1]
    return pl.pallas_call(
        _norm_inproj_kernel,
        grid=(n // tm, p // tn),
        in_specs=[pl.BlockSpec((tm, D_MODEL), lambda i, j: (i, 0)),
                  pl.BlockSpec((1, D_MODEL), lambda i, j: (0, 0)),
                  _mod_spec(layer, 1, seq, ctx_row, tm),
                  _mod_spec(layer, 0, seq, ctx_row, tm),
                  pl.BlockSpec((D_MODEL, tn), lambda i, j: (0, j)),
                  pl.BlockSpec((D_MODEL, ps), lambda i, j: (0, 0))],
        out_specs=[pl.BlockSpec((tm, tn), lambda i, j: (i, j)),
                   pl.BlockSpec((tm, ps), lambda i, j: (i, 0))],
        out_shape=[jax.ShapeDtypeStruct((n, p), BF16), jax.ShapeDtypeStruct((n, ps), F32)],
        scratch_shapes=[pltpu.VMEM((tm, D_MODEL), BF16)],
        compiler_params=_params("arbitrary", "arbitrary"),
        name="norm_inproj",
    )(x, g, mod, mod, w_main, w_small)


def _outproj_kernel(yg_ref, ym_ref, wg_ref, wm_ref, x_ref, gt_ref, g2_ref, sc_ref, sh_ref, x1_ref, h2_ref):
    acc = _mm(yg_ref[...], wg_ref[...]) + _mm(ym_ref[...], wm_ref[...])
    x1 = x_ref[...] + gt_ref[...] * acc
    x1_ref[...] = x1
    h2_ref[...] = (_rms(x1, g2_ref[...]) * (1.0 + sc_ref[...]) + sh_ref[...]).astype(h2_ref.dtype)


def _outproj(yg, ym, w_out, x, g2, mod, layer, seq, ctx_row, tm):
    n = x.shape[0]
    half = w_out.shape[0] // 2
    row = lambda i: (i, 0)
    return pl.pallas_call(
        _outproj_kernel,
        grid=(n // tm,),
        in_specs=[pl.BlockSpec((tm, half), row),
                  pl.BlockSpec((tm, half), row),
                  pl.BlockSpec((half, D_MODEL), lambda i: (0, 0)),
                  pl.BlockSpec((half, D_MODEL), lambda i: (1, 0)),
                  pl.BlockSpec((tm, D_MODEL), row),
                  _mod_spec(layer, 2, seq, ctx_row, tm),
                  pl.BlockSpec((1, D_MODEL), lambda i: (0, 0)),
                  _mod_spec(layer, 4, seq, ctx_row, tm),
                  _mod_spec(layer, 3, seq, ctx_row, tm)],
        out_specs=[pl.BlockSpec((tm, D_MODEL), row), pl.BlockSpec((tm, D_MODEL), row)],
        out_shape=[jax.ShapeDtypeStruct((n, D_MODEL), F32), jax.ShapeDtypeStruct((n, D_MODEL), BF16)],
        compiler_params=_params("arbitrary"),
        name="outproj",
    )(yg, ym, w_out, w_out, x, mod, g2, mod, mod)


def _upconv_kernel(h_ref, wg_ref, wv_ref, cw_ref, cb_ref, act_ref, ug_ref, uv_ref, *, grid_w, vertical, chunk_rows):
    n_tok, tf = ug_ref.shape
    n_rows = n_tok // grid_w
    rows_per_chunk = chunk_rows // grid_w
    n_chunks = n_tok // chunk_rows
    col = lax.broadcasted_iota(jnp.int32, (grid_w, LANE), 0)

    def project(k):
        rows = slice(k * chunk_rows, (k + 1) * chunk_rows)
        ug_ref[rows, :] = _mm(h_ref[rows, :], wg_ref[...]).astype(ug_ref.dtype)
        uv_ref[rows, :] = _mm(h_ref[rows, :], wv_ref[...]).astype(uv_ref.dtype)

    def conv(k):
        for c in range(tf // LANE):
            lanes = slice(c * LANE, (c + 1) * LANE)
            cw = cw_ref[:, lanes]
            cb = cb_ref[:, lanes]
            for r in range(k * rows_per_chunk, (k + 1) * rows_per_chunk):
                rows = slice(r * grid_w, (r + 1) * grid_w)
                mid = ug_ref[rows, lanes].astype(F32)
                taps = [cw[3 + j:4 + j] * mid for j in range(3)]
                if vertical and r > 0:
                    up = ug_ref[(r - 1) * grid_w:r * grid_w, lanes].astype(F32)
                    taps = [t + cw[j:j + 1] * up for j, t in enumerate(taps)]
                if vertical and r < n_rows - 1:
                    down = ug_ref[(r + 1) * grid_w:(r + 2) * grid_w, lanes].astype(F32)
                    taps = [t + cw[6 + j:7 + j] * down for j, t in enumerate(taps)]
                left = jnp.where(col != 0, pltpu.roll(taps[0], 1, 0), 0.0)
                right = jnp.where(col != grid_w - 1, pltpu.roll(taps[2], grid_w - 1, 0), 0.0)
                g = taps[1] + left + right + cb
                half_g = 0.5 * g
                silu = half_g * (1.0 + jnp.tanh(half_g))
                act_ref[rows, lanes] = (silu * uv_ref[rows, lanes].astype(F32)).astype(act_ref.dtype)

    project(0)
    for k in range(1, n_chunks):
        project(k)
        conv(k - 1)
    conv(n_chunks - 1)


def _upconv(h, w_up, conv_w9, conv_b, img_tokens, grid_w, vertical, chunk_rows):
    n = h.shape[0]
    tf = UPCONV_TILE
    nf = D_FF_PAD // tf
    return pl.pallas_call(
        functools.partial(_upconv_kernel, grid_w=grid_w, vertical=vertical, chunk_rows=chunk_rows),
        grid=(n // img_tokens, nf),
        in_specs=[pl.BlockSpec((img_tokens, D_MODEL), lambda b, f: (b, 0), pipeline_mode=pl.Buffered(1)),
                  pl.BlockSpec((D_MODEL, tf), lambda b, f: (0, f)),
                  pl.BlockSpec((D_MODEL, tf), lambda b, f: (0, nf + f)),
                  pl.BlockSpec((9, tf), lambda b, f: (0, f)),
                  pl.BlockSpec((1, tf), lambda b, f: (0, f))],
        out_specs=pl.BlockSpec((img_tokens, tf), lambda b, f: (b, f)),
        out_shape=jax.ShapeDtypeStruct((n, D_FF_PAD), BF16),
        scratch_shapes=[pltpu.VMEM((img_tokens, tf), BF16), pltpu.VMEM((img_tokens, tf), BF16)],
        compiler_params=_params("arbitrary", "arbitrary"),
        name="upconv",
    )(h, w_up, w_up, conv_w9, conv_b)


def _down_kernel(*refs, final):
    act_ref, wd_ref, x1_ref, gt_ref, gn_ref = refs[:5]
    if final:
        out_ref, acc_ref = refs[5:]
    else:
        sc_ref, sh_ref, x2_ref, hn_ref, acc_ref = refs[5:]
    f = pl.program_id(1)

    @pl.when(f == 0)
    def _():
        acc_ref[...] = jnp.zeros_like(acc_ref)

    acc_ref[...] += _mm(act_ref[...], wd_ref[...])

    @pl.when(f == pl.num_programs(1) - 1)
    def _():
        x2 = x1_ref[...] + gt_ref[...] * acc_ref[...]
        if final:
            out_ref[...] = _rms(x2, gn_ref[...])
        else:
            x2_ref[...] = x2
            hn_ref[...] = (_rms(x2, gn_ref[...]) * (1.0 + sc_ref[...]) + sh_ref[...]).astype(hn_ref.dtype)


def _down(act, w_down, x1, mod, layer, gn, seq, ctx_row, final, tm):
    n = x1.shape[0]
    tk = FF_TILE
    row = lambda i, f: (i, 0)
    in_specs = [pl.BlockSpec((tm, tk), lambda i, f: (i, f)),
                pl.BlockSpec((tk, D_MODEL), lambda i, f: (f, 0)),
                pl.BlockSpec((tm, D_MODEL), row),
                _mod_spec(layer, 5, seq, ctx_row, tm),
                pl.BlockSpec((1, D_MODEL), lambda i, f: (0, 0))]
    args = [act, w_down, x1, mod, gn]
    if final:
        out_specs = pl.BlockSpec((tm, D_MODEL), row)
        out_shape = jax.ShapeDtypeStruct((n, D_MODEL), F32)
    else:
        in_specs += [_mod_spec(layer + 1, 1, seq, ctx_row, tm),
                     _mod_spec(layer + 1, 0, seq, ctx_row, tm)]
        args += [mod, mod]
        out_specs = [pl.BlockSpec((tm, D_MODEL), row), pl.BlockSpec((tm, D_MODEL), row)]
        out_shape = [jax.ShapeDtypeStruct((n, D_MODEL), F32), jax.ShapeDtypeStruct((n, D_MODEL), BF16)]
    return pl.pallas_call(
        functools.partial(_down_kernel, final=final),
        grid=(n // tm, D_FF_PAD // tk),
        in_specs=in_specs,
        out_specs=out_specs,
        out_shape=out_shape,
        scratch_shapes=[pltpu.VMEM((tm, D_MODEL), F32)],
        compiler_params=_params("arbitrary", "arbitrary"),
        name="down",
    )(*args)


def _decay_kernel(zs_ref, wlr_ref, blr_ref, la_ref):
    glr = zs_ref[:, 0:GLA_LR]
    for d in range(N_DIR):
        la_ref[d] = _log_sigmoid(_mm3(glr, wlr_ref[d]) + blr_ref[d]) * (1.0 / GLA_NORMALIZER)


def _decay(zs, w_lr, b_lr, tm):
    n = zs.shape[0]
    width = w_lr.shape[-1]
    return pl.pallas_call(
        _decay_kernel,
        grid=(n // tm,),
        in_specs=[pl.BlockSpec((tm, zs.shape[1]), lambda i: (i, 0)),
                  pl.BlockSpec(w_lr.shape, lambda i: (0, 0, 0)),
                  pl.BlockSpec(b_lr.shape, lambda i: (0, 0, 0))],
        out_specs=pl.BlockSpec((N_DIR, tm, width), lambda i: (0, i, 0)),
        out_shape=jax.ShapeDtypeStruct((N_DIR, n, width), F32),
        compiler_params=_params("arbitrary"),
        name="decay",
    )(zs, w_lr, b_lr)


def _flip2(a):
    return a[..., ::-1, ::-1]


def _gla_constants():
    c = CHUNK
    m = np.zeros((N_LEVELS + 2, c, c), np.float32)
    later = np.zeros((N_LEVELS, c, 1), np.float32)
    pair = np.zeros((N_LEVELS + 1, c, c), np.float32)
    for lvl in range(N_LEVELS):
        n = c >> lvl
        half = n // 2
        for r in range(c):
            bnd = (r // n) * n + half - 1
            if r % n >= half:
                m[lvl, r, bnd + 1:r + 1] = 1.0
                later[lvl, r] = 1.0
            else:
                m[lvl, r, r + 1:bnd + 1] = 1.0
        blk = np.arange(c) // n
        pos = np.arange(c) % n
        pair[lvl] = (blk[:, None] == blk[None, :]) & (pos[:, None] >= half) & (pos[None, :] < half)
    pair[N_LEVELS] = np.eye(c)
    for r in range(c):
        m[N_LEVELS, r, :r + 1] = 1.0
        m[N_LEVELS + 1, r, r + 1:] = 1.0
    m = np.stack([m, _flip2(m)]).reshape(N_DIR, (N_LEVELS + 2) * c, c)
    later = np.stack([later, later[:, ::-1]])
    later = np.broadcast_to(later, (N_DIR, N_LEVELS, c, DK))
    pair = np.stack([pair, _flip2(pair)])
    return jnp.asarray(m, BF16), jnp.asarray(later, F32), jnp.asarray(pair, F32)


def _mlstm_constants():
    tri = np.tril(np.ones((CHUNK, CHUNK), np.float32))
    return jnp.asarray(np.stack([tri, tri.T]), F32), jnp.asarray(np.eye(CHUNK), F32)


def _drive_scan(n_ctx, n_lat, with_ctx_out, streams):
    def one(stream, d, is_ctx, s, n_total, second_half, want_out):
        chunk, finish, lat_io, ctx_io = stream
        gate_ref, y_ref = ctx_io if is_ctx else lat_io
        n = s if d == 0 else n_total - 1 - s
        start = n * CHUNK if isinstance(n, int) else pl.multiple_of(n * CHUNK, CHUNK)
        rows = pl.ds(start, CHUNK)
        o = yield from chunk(d, is_ctx, n, rows, want_out)
        if not want_out:
            return
        if second_half:
            y_ref[rows, :] = finish(y_ref[rows, :].astype(F32) + o, gate_ref[rows, :])
        else:
            y_ref[rows, :] = o.astype(y_ref.dtype)

    def steps(is_ctx, ss, n_total, second_half, want_out):
        waiting = [[one(stream, d, is_ctx, s, n_total, second_half, want_out)
                    for stream in streams for d in range(N_DIR)] for s in ss]
        live = []
        while live or waiting:
            if waiting:
                live.extend(waiting.pop(0))
            for g in list(live):
                try:
                    next(g)
                except StopIteration:
                    live.remove(g)

    assert n_ctx % 2 == 0 and n_lat % (2 * SCAN_UNROLL) == 0
    for s in range(0, n_ctx, n_ctx // 2):
        steps(True, [s + u for u in range(n_ctx // 2)], n_ctx, s >= n_ctx // 2, with_ctx_out)
    per_half = n_lat // (2 * SCAN_UNROLL)
    for half in range(2):
        def body(k, carry, half=half):
            steps(False, [k * SCAN_UNROLL + u for u in range(SCAN_UNROLL)], n_lat, half == 1, True)
            return carry

        lax.fori_loop(half * per_half, (half + 1) * per_half, body, 0)


def _gla_chunk(d, qc, kc, vc, la, seg_ref, later_ref, pair_ref, st_ref, want_out):
    c = CHUNK
    la_hi, la_lo = _split(la)
    seg = seg_ref[d]
    e_all = jnp.exp(_mm(seg, la_hi) + _mm(seg, la_lo))
    yield
    e_cum = e_all[N_LEVELS * c:(N_LEVELS + 1) * c]
    e_end = e_all[(N_LEVELS + 1) * c:(N_LEVELS + 2) * c]
    last = c - 1 if d == 0 else 0
    decay = e_cum[last:last + 1]
    qf = qc.astype(F32) * (DK ** -0.5)
    kf = kc.astype(F32)
    out = None
    if want_out:
        a = pair_ref[d, N_LEVELS] * _mm(qf.astype(BF16), kc, NT)
        for lvl in range(N_LEVELS):
            e = e_all[lvl * c:(lvl + 1) * c]
            later = later_ref[d, lvl]
            ql = (qf * e * later).astype(BF16)
            kl = (kf * e * (1.0 - later)).astype(BF16)
            a = a + pair_ref[d, lvl] * _mm(ql, kl, NT)
            yield
    st = st_ref[d]
    if want_out:
        out = _mm(a.astype(BF16), vc) + _mm((qf * e_cum).astype(BF16), st.astype(BF16), NT)
    st_ref[d] = st * decay + _mm(vc, (kf * e_end).astype(BF16), TN)
    return out


N_GLA_IN = 14


def _gla_stream(in_refs, y_ref, cy_ref, st_ref):
    (q_ref, k_ref, v_ref, g_ref, la_ref, cq_ref, ck_ref, cv_ref, cg_ref, cla_ref,
     gn_ref, seg_ref, later_ref, pair_ref) = in_refs

    def finish(tot, gate):
        gate = gate.astype(F32)
        return (_rms(tot, gn_ref[...]) * (gate * _sigmoid(gate))).astype(BF16)

    def chunk(d, is_ctx, n, rows, want_out):
        del n
        qr, kr, vr, lr = (cq_ref, ck_ref, cv_ref, cla_ref) if is_ctx else (q_ref, k_ref, v_ref, la_ref)
        return (yield from _gla_chunk(d, qr[rows, :], kr[rows, :], vr[rows, :], lr[d, rows, :],
                                      seg_ref, later_ref, pair_ref, st_ref, want_out))

    st_ref[...] = jnp.zeros_like(st_ref)
    return chunk, finish, (g_ref, y_ref), (cg_ref, cy_ref)


def _gla_specs(t, tc, consts):
    seg, later, pair = consts
    lat = lambda cols, off: pl.BlockSpec((t, cols), lambda b, h, off=off: (b, off + h))
    ctx = lambda cols, off: pl.BlockSpec((tc, cols), lambda b, h, off=off: (b, off + h))
    const = lambda shape: pl.BlockSpec(shape, lambda b, h: (0,) * len(shape))
    return [lat(DK, 0), lat(DK, HEADS), lat(DV, HEADS), lat(DV, 2 * HEADS),
            pl.BlockSpec((N_DIR, t, DK), lambda b, h: (0, b, h)),
            ctx(DK, 0), ctx(DK, HEADS), ctx(DV, HEADS), ctx(DV, 2 * HEADS),
            pl.BlockSpec((N_DIR, tc, DK), lambda b, h: (0, b, h)),
            const((1, DV)), const(seg.shape), const(later.shape), const(pair.shape)]


def _mlstm_chunk(d, qc, kc, vc, gates, tri, tri_t, eye, ct_ref, n_ref, m_ref, want_out):
    li_row = gates[2 * d:2 * d + 1]
    lf_row = _log_sigmoid(gates[2 * d + 1:2 * d + 2])
    li_col = jnp.sum(eye * li_row, axis=1, keepdims=True)
    lf_col = jnp.sum(eye * lf_row, axis=1, keepdims=True)
    fcum_col = jnp.sum(tri * lf_row, axis=1, keepdims=True)
    fcum_row = jnp.sum(tri_t * lf_col, axis=0, keepdims=True)
    f_end = jnp.sum(lf_row, axis=1, keepdims=True)
    yield
    m = m_ref[d][:, 0:1]
    m_new = jnp.maximum(f_end + m, jnp.max(f_end - fcum_row + li_row, axis=1, keepdims=True))
    carry = jnp.exp(f_end + m - m_new)
    w_col = jnp.exp(f_end - fcum_col + li_col - m_new)
    m_ref[d] = jnp.broadcast_to(m_new, m_ref.shape[1:])
    kf = kc.astype(F32) * (DK ** -0.5)
    kw = kf * w_col
    yield
    out = None
    if want_out:
        qf = qc.astype(F32)
        d_log = fcum_col - fcum_row + li_row
        causal = tri > 0.5
        inter_log = fcum_col + m
        m_q = jnp.maximum(inter_log, jnp.max(jnp.where(causal, d_log, -jnp.inf), axis=1, keepdims=True))
        p = jnp.where(causal, jnp.exp(d_log - m_q), 0.0)
        s = _mm(qc, kf.astype(BF16), NT) * p
        inter = jnp.exp(inter_log - m_q)
        yield
    ct = ct_ref[d]
    nvec = n_ref[d]
    if want_out:
        num = _mm(s.astype(BF16), vc) + inter * _mm(qc, ct.astype(BF16), NT)
        den = jnp.sum(s, axis=1, keepdims=True) + inter * jnp.sum(qf * nvec, axis=1, keepdims=True)
        out = num / jnp.maximum(jnp.abs(den), jnp.exp(-m_q))
    ct_ref[d] = carry * ct + _mm(vc, kw.astype(BF16), TN)
    n_ref[d] = carry * nvec + jnp.sum(kw, axis=0, keepdims=True)
    return out


N_MLSTM_IN = 14


def _mlstm_stream(in_refs, y_ref, cy_ref, ct_ref, n_ref, m_ref):
    (q_ref, k_ref, v_ref, g_ref, gr_ref, cq_ref, ck_ref, cv_ref, cg_ref, cgr_ref,
     bias_ref, gn_ref, tri_ref, eye_ref) = in_refs

    def finish(tot, gate):
        return (_sigmoid(gate.astype(F32)) * _rms(tot, gn_ref[...])).astype(BF16)

    def chunk(d, is_ctx, n, rows, want_out):
        qr, kr, vr, grr = (cq_ref, ck_ref, cv_ref, cgr_ref) if is_ctx else (q_ref, k_ref, v_ref, gr_ref)
        return (yield from _mlstm_chunk(d, qr[rows, :], kr[rows, :], vr[rows, :], grr[n] + bias_ref[...],
                                        tri_ref[d], tri_ref[1 - d], eye_ref[...], ct_ref, n_ref, m_ref, want_out))

    ct_ref[...] = jnp.zeros_like(ct_ref)
    n_ref[...] = jnp.zeros_like(n_ref)
    m_ref[...] = jnp.zeros_like(m_ref)
    return chunk, finish, (g_ref, y_ref), (cg_ref, cy_ref)


def _gate_rows(zs):
    n = zs.shape[0]
    g = zs[:, GLA_LR:].reshape(n // CHUNK, CHUNK, N_DIR * 2, HEADS)
    return jnp.transpose(g, (3, 0, 2, 1))


def _mlstm_specs(t, tc, consts):
    tri, eye = consts
    base = (2 * HEADS * DK + 2 * HEADS * DV)
    bq, bv = base // DK, (base + 2 * HEADS * DK) // DV
    lat = lambda cols, off: pl.BlockSpec((t, cols), lambda b, h, off=off: (b, off + h))
    ctx = lambda cols, off: pl.BlockSpec((tc, cols), lambda b, h, off=off: (b, off + h))
    const = lambda shape: pl.BlockSpec(shape, lambda b, h: (0,) * len(shape))
    return [lat(DK, bq), lat(DK, bq + HEADS), lat(DV, bv), lat(DV, bv + HEADS),
            pl.BlockSpec((None, t // CHUNK, 2 * N_DIR, CHUNK), lambda b, h: (h, b, 0, 0)),
            ctx(DK, bq), ctx(DK, bq + HEADS), ctx(DV, bv), ctx(DV, bv + HEADS),
            pl.BlockSpec((None, tc // CHUNK, 2 * N_DIR, CHUNK), lambda b, h: (h, b, 0, 0)),
            pl.BlockSpec((None, 2 * N_DIR, 1), lambda b, h: (h, 0, 0)),
            const((1, DV)), const(tri.shape), const(eye.shape)]


def _scan_kernel(*refs, with_ctx_out):
    gla_in = refs[:N_GLA_IN]
    ml_in = refs[N_GLA_IN:N_GLA_IN + N_MLSTM_IN]
    rest = refs[N_GLA_IN + N_MLSTM_IN:]
    if with_ctx_out:
        yg_ref, ym_ref, cyg_ref, cym_ref, st_ref, ct_ref, n_ref, m_ref = rest
    else:
        yg_ref, ym_ref, st_ref, ct_ref, n_ref, m_ref = rest
        cyg_ref = cym_ref = None
    streams = [_gla_stream(gla_in, yg_ref, cyg_ref, st_ref),
               _mlstm_stream(ml_in, ym_ref, cym_ref, ct_ref, n_ref, m_ref)]
    _drive_scan(gla_in[5].shape[0] // CHUNK, gla_in[0].shape[0] // CHUNK, with_ctx_out, streams)


def _scans(z, zs, la, cz, czs, cla, gla_g_norm, b_gate, mlstm_g_norm, gla_consts, mlstm_consts, batch,
           with_ctx_out):
    t = z.shape[0] // batch
    tc = cz.shape[0] // batch
    in_specs = _gla_specs(t, tc, gla_consts) + _mlstm_specs(t, tc, mlstm_consts)
    head_out = lambda rows: pl.BlockSpec((rows, DV), lambda b, h: (b, h))
    out_specs = [head_out(t), head_out(t)]
    out_shape = [jax.ShapeDtypeStruct((batch * t, HEADS * DV), BF16)] * 2
    scratch = [pltpu.VMEM((N_DIR, DV, DK), F32), pltpu.VMEM((N_DIR, DV, DK), F32),
               pltpu.VMEM((N_DIR, 1, DK), F32), pltpu.VMEM((N_DIR, 1, DK), F32)]
    if with_ctx_out:
        out_specs += [head_out(tc), head_out(tc)]
        out_shape += [jax.ShapeDtypeStruct((batch * tc, HEADS * DV), BF16)] * 2
    bias = jnp.transpose(b_gate.reshape(2 * N_DIR, HEADS))[:, :, None]
    return pl.pallas_call(
        functools.partial(_scan_kernel, with_ctx_out=with_ctx_out),
        grid=(batch, HEADS),
        in_specs=in_specs,
        out_specs=out_specs,
        out_shape=out_shape,
        scratch_shapes=scratch,
        compiler_params=_params("arbitrary", "arbitrary"),
        name="scans",
    )(z, z, z, z, la, cz, cz, cz, cz, cla, gla_g_norm, *gla_consts,
      z, z, z, z, _gate_rows(zs), cz, cz, cz, cz, _gate_rows(czs), bias, mlstm_g_norm, *mlstm_consts)


def _prep_layer_weights(w_in, w_out, w_up, conv_w, conv_b, w_down):
    hk, hv = HEADS * DK, HEADS * DV
    glr0 = 2 * hk + 2 * hv
    ml0 = glr0 + GLA_LR
    gate0 = ml0 + 2 * hk + 2 * hv
    w_main = jnp.concatenate([w_in[:, :glr0], w_in[:, ml0:gate0]], axis=1).astype(BF16)
    w_small = jnp.concatenate([w_in[:, glr0:ml0], w_in[:, gate0:]], axis=1).astype(BF16)
    pad = D_FF_PAD - D_FF
    zc = jnp.zeros((D_MODEL, pad), w_up.dtype)
    w_up_p = jnp.concatenate([w_up[:, :D_FF], zc, w_up[:, D_FF:], zc], axis=1).astype(BF16)
    conv_w9 = jnp.pad(conv_w.reshape(9, D_FF), ((0, 0), (0, pad)))
    conv_bp = jnp.pad(conv_b.reshape(1, D_FF), ((0, 0), (0, pad)))
    w_down_p = jnp.pad(w_down, ((0, pad), (0, 0))).astype(BF16)
    return w_main, w_small, w_out.astype(BF16), w_up_p, conv_w9, conv_bp, w_down_p


def kernel(x, c, ctx, c_ctx, w_mod, b_mod, g_norm1, g_norm2, w_in, gla_w_lr, gla_b_lr, mlstm_b_gate,
           gla_g_norm, mlstm_g_norm, w_out, w_up, conv_w, conv_b, w_down, g_final):
    batch, seq, _ = x.shape
    ctx_len = ctx.shape[1]
    assert batch < MOD_ROWS and seq == GRID_W * GRID_W and ctx_len % CHUNK == 0
    tm_lat = 512
    tm_ctx = min(512, batch * ctx_len)
    cc = jnp.zeros((MOD_ROWS, D_MODEL), F32).at[:batch].set(c).at[batch].set(c_ctx)
    mod = _modulation(cc, w_mod, b_mod)
    gla_consts = _gla_constants()
    mlstm_consts = _mlstm_constants()

    xl = x.reshape(batch * seq, D_MODEL)
    xc = ctx.reshape(batch * ctx_len, D_MODEL)
    row = lambda v: v.reshape(1, -1)
    hl = hc = out = None
    for l in range(DEPTH):
        last = l == DEPTH - 1
        w_main, w_small, w_out_b, w_up_p, conv_w9, conv_bp, w_down_p = _prep_layer_weights(
            w_in[l], w_out[l], w_up[l], conv_w[l], conv_b[l], w_down[l])
        if l == 0:
            z, zs = _norm_inproj(xl, row(g_norm1[0]), mod, 0, seq, None, w_main, w_small, 1024, 1024)
            cz, czs = _norm_inproj(xc, row(g_norm1[0]), mod, 0, tm_ctx, batch, w_main, w_small, tm_ctx, 1024)
        else:
            z, zs = _inproj(hl, w_main, w_small, 1024, 1024)
            cz, czs = _inproj(hc, w_main, w_small, tm_ctx, 1024)
        b_lr = gla_b_lr[l].reshape(N_DIR, 1, HEADS * DK)
        la = _decay(zs, gla_w_lr[l], b_lr, 1024)
        cla = _decay(czs, gla_w_lr[l], b_lr, tm_ctx)
        ys = _scans(z, zs, la, cz, czs, cla, row(gla_g_norm[l]), mlstm_b_gate[l],
                    row(mlstm_g_norm[l]), gla_consts, mlstm_consts, batch, not last)
        x1, h2 = _outproj(ys[0], ys[1], w_out_b, xl, row(g_norm2[l]), mod, l, seq, None, 512)
        act = _upconv(h2, w_up_p, conv_w9, conv_bp, seq, GRID_W, True, 256)
        if last:
            out = _down(act, w_down_p, x1, mod, l, row(g_final), seq, None, True, tm_lat)
        else:
            xl, hl = _down(act, w_down_p, x1, mod, l, row(g_norm1[l + 1]), seq, None, False, tm_lat)
            c1, hc2 = _outproj(ys[2], ys[3], w_out_b, xc, row(g_norm2[l]), mod, l, 256, batch, 256)
            cact = _upconv(hc2, w_up_p, conv_w9, conv_bp, batch * ctx_len, ctx_len, False, ctx_len)
            xc, hc = _down(cact, w_down_p, c1, mod, l, row(g_norm1[l + 1]), tm_ctx, batch, False, tm_ctx)
    return out.reshape(batch, seq, D_MODEL)
```
